```python
import jax, jax.numpy as jnp
from jax import lax
import numpy as np

D_MODEL = 1024
BATCH = 8
SEQ = 4096
DEPTH = 1

D_POOL = D_MODEL // 2
POOL_WINDOWS = (2, 4, 8, 16)
POOL_GROUP = D_POOL // len(POOL_WINDOWS)
D_ATTN = D_MODEL - D_POOL
HEAD_DIM = 64
N_HEADS = D_ATTN // HEAD_DIM
ROT_DIM = HEAD_DIM // 4
ROPE_THETA = 500000.0
MOBA_BLOCK = 256
MOBA_TOPK = 3
Q_CHUNK = 32
D_IN = 2 * D_POOL + 4 * D_ATTN
ALPHA = (2.0 * DEPTH) ** 0.25
BETA = (8.0 * DEPTH) ** -0.25
LN_EPS = 1e-5

kernel_name = "hymba_pool_moba_deepnorm"


def layer_norm(x, gain, bias):
    xf = x.astype(jnp.float32)
    mu = jnp.mean(xf, axis=-1, keepdims=True)
    var = jnp.mean(jnp.square(xf - mu), axis=-1, keepdims=True)
    return ((xf - mu) * lax.rsqrt(var + LN_EPS) * gain + bias).astype(x.dtype)


def partial_rotary(x, positions):
    half = ROT_DIM // 2
    freqs = ROPE_THETA ** (-jnp.arange(half, dtype=jnp.float32) * 2.0 / ROT_DIM)
    ang = positions.astype(jnp.float32)[..., None] * freqs
    cos = jnp.cos(ang)[:, :, None, :]
    sin = jnp.sin(ang)[:, :, None, :]
    xr = x[..., :ROT_DIM].astype(jnp.float32)
    x1, x2 = xr[..., :half], xr[..., half:]
    rot = jnp.concatenate([x1 * cos - x2 * sin, x2 * cos + x1 * sin], axis=-1)
    return jnp.concatenate([rot.astype(x.dtype), x[..., ROT_DIM:]], axis=-1)


def multiscale_pool(u, pool_w, pool_scale):
    b, s, _ = u.shape
    uf = u.astype(jnp.float32)
    cs = jnp.cumsum(uf, axis=1)
    t = jnp.arange(s)
    diffs = []
    for g, w in enumerate(POOL_WINDOWS):
        sl = slice(g * POOL_GROUP, (g + 1) * POOL_GROUP)
        cg = cs[..., sl]
        prev = jnp.pad(cg, ((0, 0), (w, 0), (0, 0)))[:, :s]
        count = jnp.minimum(t + 1, w).astype(jnp.float32)[None, :, None]
        diffs.append((cg - prev) / count - uf[..., sl])
    d = jnp.stack(diffs, axis=2).astype(u.dtype)
    y = jnp.einsum('bsgc,gcd->bsgd', d, pool_w).reshape(b, s, D_POOL)
    return y * pool_scale


def moba_attention(q, k, v):
    b, s, h, d = q.shape
    nb = -(-s // MOBA_BLOCK)
    pad = nb * MOBA_BLOCK - s
    qh = q.transpose(0, 2, 1, 3)
    kh = jnp.pad(k.transpose(0, 2, 1, 3), ((0, 0), (0, 0), (0, pad), (0, 0)))
    vh = jnp.pad(v.transpose(0, 2, 1, 3), ((0, 0), (0, 0), (0, pad), (0, 0)))
    k_blocks = kh.reshape(b, h, nb, MOBA_BLOCK, d)
    v_blocks = vh.reshape(b, h, nb, MOBA_BLOCK, d)
    k_mean = jnp.mean(k_blocks.astype(jnp.float32), axis=3)
    n_sel = min(MOBA_TOPK, nb)
    scale = HEAD_DIM ** -0.5
    bi = jnp.arange(b)[:, None, None, None]
    hi = jnp.arange(h)[None, :, None, None]
    blk = jnp.arange(nb)
    kpos = jnp.arange(MOBA_BLOCK)
    qoff = jnp.arange(Q_CHUNK)

    def chunk(c):
        start = c * Q_CHUNK
        cur = start // MOBA_BLOCK
        qc = lax.dynamic_slice_in_dim(qh, start, Q_CHUNK, axis=2)
        qpos = start + qoff
        gate = jnp.einsum('bhqd,bhnd->bhqn', qc.astype(jnp.float32), k_mean)
        gate = jnp.where(blk < cur, gate, -jnp.inf)
        _, idx = lax.top_k(gate, n_sel)
        valid = idx < cur
        kg = k_blocks[bi, hi, idx]
        vg = v_blocks[bi, hi, idx]
        s_past = jnp.einsum('bhqd,bhqnkd->bhqnk', qc, kg).astype(jnp.float32) * scale
        s_past = jnp.where(valid[..., None], s_past, -jnp.inf)
        s_past = s_past.reshape(b, h, Q_CHUNK, n_sel * MOBA_BLOCK)
        k_own = lax.dynamic_slice_in_dim(kh, cur * MOBA_BLOCK, MOBA_BLOCK, axis=2)
        v_own = lax.dynamic_slice_in_dim(vh, cur * MOBA_BLOCK, MOBA_BLOCK, axis=2)
        s_own = jnp.einsum('bhqd,bhkd->bhqk', qc, k_own).astype(jnp.float32) * scale
        causal = (cur * MOBA_BLOCK + kpos)[None, :] <= qpos[:, None]
        s_own = jnp.where(causal, s_own, -jnp.inf)
        p = jax.nn.softmax(jnp.concatenate([s_past, s_own], axis=-1), axis=-1).astype(v.dtype)
        p_past = p[..., :n_sel * MOBA_BLOCK].reshape(b, h, Q_CHUNK, n_sel, MOBA_BLOCK)
        p_own = p[..., n_sel * MOBA_BLOCK:]
        return (jnp.einsum('bhqnk,bhqnkd->bhqd', p_past, vg)
                + jnp.einsum('bhqk,bhkd->bhqd', p_own, v_own))

    out = lax.map(chunk, jnp.arange(s // Q_CHUNK))
    return out.transpose(1, 0, 3, 2, 4).reshape(b, s, h * d)


def setup_inputs(seed: int = 0) -> dict:
    key = jax.random.key(seed)
    ks = jax.random.split(key, 8)
    x = jax.random.normal(ks[0], (BATCH, SEQ, D_MODEL), jnp.float32)
    positions = jnp.broadcast_to(jnp.arange(SEQ, dtype=jnp.int32), (BATCH, SEQ))
    w_in = jax.random.normal(ks[1], (DEPTH, D_MODEL, D_IN), jnp.float32) * D_MODEL ** -0.5
    pool_w = jax.random.normal(ks[2], (DEPTH, len(POOL_WINDOWS), POOL_GROUP, POOL_GROUP), jnp.float32) * POOL_GROUP ** -0.5
    pool_scale = 1.0 + 0.02 * jax.random.normal(ks[3], (DEPTH, D_POOL), jnp.float32)
    w_out = jax.random.normal(ks[4], (DEPTH, D_MODEL, D_MODEL), jnp.float32) * (D_MODEL ** -0.5) * BETA
    ln_gain = 1.0 + 0.02 * jax.random.normal(ks[5], (DEPTH, D_MODEL), jnp.float32)
    ln_bias = 0.02 * jax.random.normal(ks[6], (DEPTH, D_MODEL), jnp.float32)
    return {'x': x, 'positions': positions, 'w_in': w_in, 'pool_w': pool_w,
            'pool_scale': pool_scale, 'w_out': w_out, 'ln_gain': ln_gain, 'ln_bias': ln_bias}


def reference(x, positions, w_in, pool_w, pool_scale, w_out, ln_gain, ln_bias):
    b, s, _ = x.shape
    h = x
    cuts = [D_POOL, 2 * D_POOL, 2 * D_POOL + D_ATTN, 2 * D_POOL + 2 * D_ATTN, 2 * D_POOL + 3 * D_ATTN]
    for layer in range(DEPTH):
        proj = jnp.einsum('bsd,de->bse', h, w_in[layer])
        u_pool, g_pool, q, k, v, g_attn = jnp.split(proj, cuts, axis=-1)
        y_pool = multiscale_pool(u_pool, pool_w[layer], pool_scale[layer]) * jax.nn.silu(g_pool)
        q = partial_rotary(q.reshape(b, s, N_HEADS, HEAD_DIM), positions)
        k = partial_rotary(k.reshape(b, s, N_HEADS, HEAD_DIM), positions)
        v = v.reshape(b, s, N_HEADS, HEAD_DIM)
        y_attn = moba_attention(q, k, v) * jax.nn.silu(g_attn)
        mix = jnp.concatenate([y_pool, y_attn], axis=-1)
        out = jnp.einsum('bse,ed->bsd', mix, w_out[layer])
        h = layer_norm(ALPHA * h + out, ln_gain[layer], ln_bias[layer])
    return h
```

```python
import functools

import jax
import jax.numpy as jnp
from jax import lax
from jax.experimental import pallas as pl
from jax.experimental.pallas import tpu as pltpu

POOL_WINDOWS = (2, 4, 8, 16)
HEAD_DIM = 64
ROT_DIM = HEAD_DIM // 4
ROT_HALF = ROT_DIM // 2
ROPE_THETA = 500000.0
MOBA_BLOCK = 256
MOBA_TOPK = 3
LN_EPS = 1e-5

LANES = 128
POOL_HALO = 16
ROW_TILE = 512
MASK_NEG = -1e30
VMEM_LIMIT = 48 * 1024 * 1024

_F32 = jnp.float32
_BF16 = jnp.bfloat16


def _silu(x):
    return x * (1.0 / (1.0 + jnp.exp(-x)))


def _dot(a, b):
    return jnp.dot(a, b, preferred_element_type=_F32)


def _dot_nt(a, b):
    return lax.dot_general(a, b, (((1,), (1,)), ((), ())), preferred_element_type=_F32)


def _rotate_heads(t, cos, sin, n_heads):
    pieces = []
    for h in range(n_heads):
        base = h * HEAD_DIM
        x1 = t[base:base + ROT_HALF]
        x2 = t[base + ROT_HALF:base + ROT_DIM]
        pieces.append(x1 * cos - x2 * sin)
        pieces.append(x2 * cos + x1 * sin)
        pieces.append(t[base + ROT_DIM:base + HEAD_DIM])
    return jnp.concatenate(pieces, axis=0)


def _proj_kernel(x_ref, pos_ref, freq_ref, wstd_ref, wt_ref, poolw_ref, pscale_ref,
                 ypool_ref, qt_ref, k_ref, vt_ref, km_ref, ga_ref, ubuf_ref,
                 *, d_pool, d_attn, n_heads, tm):
    s_idx = pl.program_id(1)
    xb = x_ref[...].astype(_BF16)

    std = _dot(xb, wstd_ref[...])
    tr = _dot_nt(wt_ref[...], xb)

    @pl.when(s_idx == 0)
    def _():
        ubuf_ref[0:POOL_HALO, :] = jnp.zeros((POOL_HALO, d_pool), _F32)

    ubuf_ref[POOL_HALO:POOL_HALO + tm, :] = std[:, 0:d_pool]
    group = d_pool // len(POOL_WINDOWS)
    t_glob = s_idx * tm + lax.broadcasted_iota(jnp.int32, (tm, group), 0)
    for g, w in enumerate(POOL_WINDOWS):
        cols = slice(g * group, (g + 1) * group)
        u_g = ubuf_ref[POOL_HALO:POOL_HALO + tm, cols]
        tsum = u_g
        for s in range(1, w):
            tsum = tsum + ubuf_ref[POOL_HALO - s:POOL_HALO - s + tm, cols]
        count = jnp.minimum(t_glob + 1, w).astype(_F32)
        d = tsum / count - u_g
        y = _dot(d.astype(_BF16), poolw_ref[g])
        gate = _silu(std[:, d_pool + g * group:d_pool + (g + 1) * group])
        ypool_ref[:, cols] = (y * pscale_ref[:, cols] * gate).astype(_BF16)
    ubuf_ref[0:POOL_HALO, :] = ubuf_ref[tm:tm + POOL_HALO, :]

    ga_ref[...] = _silu(std[:, 2 * d_pool:2 * d_pool + d_attn]).astype(_BF16)

    ang = freq_ref[...] * pos_ref[0].astype(_F32)
    cos = jnp.cos(ang)
    sin = jnp.sin(ang)

    scale = HEAD_DIM ** -0.5
    qt = _rotate_heads(tr[0:d_attn], cos, sin, n_heads) * scale
    qt = qt.astype(_BF16)
    for r in range(tm // MOBA_BLOCK):
        qt_ref[0, r] = qt[:, r * MOBA_BLOCK:(r + 1) * MOBA_BLOCK]

    kt = _rotate_heads(tr[d_attn:2 * d_attn], cos, sin, n_heads)
    for c in range(d_attn // LANES):
        kc = kt[c * LANES:(c + 1) * LANES].T
        k_ref[:, c * LANES:(c + 1) * LANES] = kc.astype(_BF16)
        for r in range(tm // MOBA_BLOCK):
            blk_sum = jnp.sum(kc[r * MOBA_BLOCK:(r + 1) * MOBA_BLOCK], axis=0, keepdims=True)
            km_ref[r, :, c * LANES:(c + 1) * LANES] = blk_sum * (1.0 / MOBA_BLOCK)

    vt = tr[2 * d_attn:3 * d_attn].astype(_BF16)
    for r in range(tm // MOBA_BLOCK):
        vt_ref[0, r] = vt[:, r * MOBA_BLOCK:(r + 1) * MOBA_BLOCK]


def _attn_kernel(qt_ref, k_ref, vt_ref, km_ref, ga_ref, o_ref, bias_ref, *, n_blocks):
    i = pl.program_id(2)
    blk_rows = MOBA_BLOCK
    qt = qt_ref[0, 0]
    zeros_half = jnp.zeros((HEAD_DIM, blk_rows), _BF16)
    qm = (jnp.concatenate([qt[0:HEAD_DIM], zeros_half], axis=0),
          jnp.concatenate([zeros_half, qt[HEAD_DIM:2 * HEAD_DIM]], axis=0))

    km = km_ref[0]
    km_hi = km.astype(_BF16)
    km_lo = (km - km_hi.astype(_F32)).astype(_BF16)
    blk = lax.broadcasted_iota(jnp.int32, (n_blocks, blk_rows), 0)
    valid = blk < i
    for hh in range(2):
        g = _dot(km_hi, qm[hh]) + _dot(km_lo, qm[hh])
        g = jnp.where(valid, g, -jnp.inf)
        beaten = jnp.zeros((n_blocks, blk_rows), jnp.int32)
        for j in range(n_blocks):
            gj = g[j:j + 1, :]
            tie = jnp.where(blk > j, 1, 0)
            beaten = beaten + jnp.where(gj > g, 1, jnp.where(gj == g, tie, 0))
        keep = jnp.where(valid, jnp.where(beaten < MOBA_TOPK, 0.0, MASK_NEG), MASK_NEG)
        for j in range(n_blocks):
            bias_ref[hh, j] = keep[j:j + 1, :]

    kd = k_ref[pl.ds(pl.multiple_of(i * blk_rows, blk_rows), blk_rows), :]
    key_pos = lax.broadcasted_iota(jnp.int32, (blk_rows, blk_rows), 0)
    qry_pos = lax.broadcasted_iota(jnp.int32, (blk_rows, blk_rows), 1)
    causal = key_pos <= qry_pos
    carry = []
    for hh in range(2):
        s = jnp.where(causal, _dot(kd, qm[hh]), MASK_NEG)
        m = jnp.max(s, axis=0, keepdims=True)
        p = jnp.exp(s - m)
        l = jnp.sum(p, axis=0, keepdims=True)
        acc = _dot(vt_ref[0, i, hh * HEAD_DIM:(hh + 1) * HEAD_DIM, :], p.astype(_BF16))
        carry += [m, l, acc]

    def body(j, carry):
        kj = k_ref[pl.ds(pl.multiple_of(j * blk_rows, blk_rows), blk_rows), :]
        out = []
        for hh in range(2):
            m, l, acc = carry[3 * hh:3 * hh + 3]
            s = _dot(kj, qm[hh]) + bias_ref[hh, j]
            m_new = jnp.maximum(m, jnp.max(s, axis=0, keepdims=True))
            alpha = jnp.exp(m - m_new)
            p = jnp.exp(s - m_new)
            l = alpha * l + jnp.sum(p, axis=0, keepdims=True)
            acc = alpha * acc + _dot(vt_ref[0, j, hh * HEAD_DIM:(hh + 1) * HEAD_DIM, :], p.astype(_BF16))
            out += [m_new, l, acc]
        return tuple(out)

    carry = lax.fori_loop(0, i, body, tuple(carry))

    ot = jnp.concatenate([carry[2] / carry[1], carry[5] / carry[4]], axis=0)
    o_ref[...] = (ot.T * ga_ref[...].astype(_F32)).astype(_BF16)


def _out_kernel(yp_ref, ya_ref, x_ref, w_ref, gain_ref, bias_ref, o_ref, *, d_pool, alpha):
    out = _dot(yp_ref[...], w_ref[0:d_pool, :]) + _dot(ya_ref[...], w_ref[d_pool:, :])
    z = alpha * x_ref[...] + out
    mu = jnp.mean(z, axis=-1, keepdims=True)
    zc = z - mu
    var = jnp.mean(zc * zc, axis=-1, keepdims=True)
    o_ref[...] = zc * lax.rsqrt(var + LN_EPS) * gain_ref[...] + bias_ref[...]


def _layer(h, positions, w_in, pool_w, pool_scale, w_out, ln_gain, ln_bias, alpha):
    b, s, d_model = h.shape
    d_pool = pool_scale.shape[0]
    d_attn = d_model - d_pool
    n_heads = d_attn // HEAD_DIM
    n_blocks = s // MOBA_BLOCK
    tm = ROW_TILE
    n_tiles = s // tm
    rows = b * s
    blocks_per_tile = tm // MOBA_BLOCK
    assert s % tm == 0 and tm % MOBA_BLOCK == 0 and d_attn % LANES == 0
    assert w_in.shape == (d_model, 2 * d_pool + 4 * d_attn)

    c_q = 2 * d_pool
    c_g = 2 * d_pool + 3 * d_attn
    w_std = jnp.concatenate([w_in[:, :c_q], w_in[:, c_g:]], axis=1).astype(_BF16)
    w_t = w_in[:, c_q:c_g].T.astype(_BF16)
    freqs = ROPE_THETA ** (-jnp.arange(ROT_HALF, dtype=_F32) * 2.0 / ROT_DIM)
    freq_tab = jnp.broadcast_to(freqs[:, None], (ROT_HALF, tm))
    x2 = h.reshape(rows, d_model)
    pos3 = positions.reshape(b, 1, s)

    proj = pl.pallas_call(
        functools.partial(_proj_kernel, d_pool=d_pool, d_attn=d_attn, n_heads=n_heads, tm=tm),
        grid=(b, n_tiles),
        in_specs=[
            pl.BlockSpec((tm, d_model), lambda bi, si: (bi * n_tiles + si, 0)),
            pl.BlockSpec((1, 1, tm), lambda bi, si: (bi, 0, si)),
            pl.BlockSpec((ROT_HALF, tm), lambda bi, si: (0, 0)),
            pl.BlockSpec(w_std.shape, lambda bi, si: (0, 0)),
            pl.BlockSpec(w_t.shape, lambda bi, si: (0, 0)),
            pl.BlockSpec(pool_w.shape, lambda bi, si: (0, 0, 0)),
            pl.BlockSpec((1, d_pool), lambda bi, si: (0, 0)),
        ],
        out_specs=[
            pl.BlockSpec((tm, d_pool), lambda bi, si: (bi * n_tiles + si, 0)),
            pl.BlockSpec((1, blocks_per_tile, d_attn, MOBA_BLOCK), lambda bi, si: (bi, si, 0, 0)),
            pl.BlockSpec((tm, d_attn), lambda bi, si: (bi * n_tiles + si, 0)),
            pl.BlockSpec((1, blocks_per_tile, d_attn, MOBA_BLOCK), lambda bi, si: (bi, si, 0, 0)),
            pl.BlockSpec((blocks_per_tile, 1, d_attn), lambda bi, si: (bi * n_tiles + si, 0, 0)),
            pl.BlockSpec((tm, d_attn), lambda bi, si: (bi * n_tiles + si, 0)),
        ],
        out_shape=[
            jax.ShapeDtypeStruct((rows, d_pool), _BF16),
            jax.ShapeDtypeStruct((b, n_blocks, d_attn, MOBA_BLOCK), _BF16),
            jax.ShapeDtypeStruct((rows, d_attn), _BF16),
            jax.ShapeDtypeStruct((b, n_blocks, d_attn, MOBA_BLOCK), _BF16),
            jax.ShapeDtypeStruct((b * n_blocks, 1, d_attn), _F32),
            jax.ShapeDtypeStruct((rows, d_attn), _BF16),
        ],
        scratch_shapes=[pltpu.VMEM((tm + POOL_HALO, d_pool), _F32)],
        compiler_params=pltpu.CompilerParams(
            dimension_semantics=("arbitrary", "arbitrary"), vmem_limit_bytes=VMEM_LIMIT),
        name="proj_pool_rope",
    )
    y_pool, q_t, k_rot, v_t, k_mean, g_attn = proj(
        x2, pos3, freq_tab, w_std, w_t, pool_w.astype(_BF16), pool_scale.reshape(1, d_pool))

    k_mean = k_mean.reshape(b, n_blocks, d_attn)
    pair = 2 * HEAD_DIM
    attn = pl.pallas_call(
        functools.partial(_attn_kernel, n_blocks=n_blocks),
        grid=(b, d_attn // pair, n_blocks),
        in_specs=[
            pl.BlockSpec((1, 1, pair, MOBA_BLOCK), lambda bi, hp, i: (bi, i, hp, 0)),
            pl.BlockSpec((s, pair), lambda bi, hp, i: (bi, hp)),
            pl.BlockSpec((1, n_blocks, pair, MOBA_BLOCK), lambda bi, hp, i: (bi, 0, hp, 0)),
            pl.BlockSpec((1, n_blocks, pair), lambda bi, hp, i: (bi, 0, hp)),
            pl.BlockSpec((MOBA_BLOCK, pair), lambda bi, hp, i: (bi * n_blocks + i, hp)),
        ],
        out_specs=pl.BlockSpec((MOBA_BLOCK, pair), lambda bi, hp, i: (bi * n_blocks + i, hp)),
        out_shape=jax.ShapeDtypeStruct((rows, d_attn), _BF16),
        scratch_shapes=[pltpu.VMEM((2, n_blocks, 1, MOBA_BLOCK), _F32)],
        compiler_params=pltpu.CompilerParams(
            dimension_semantics=("arbitrary", "arbitrary", "arbitrary"), vmem_limit_bytes=VMEM_LIMIT),
        name="moba_attention",
    )
    y_attn = attn(q_t, k_rot, v_t, k_mean, g_attn)

    outp = pl.pallas_call(
        functools.partial(_out_kernel, d_pool=d_pool, alpha=alpha),
        grid=(rows // tm,),
        in_specs=[
            pl.BlockSpec((tm, d_pool), lambda r: (r, 0)),
            pl.BlockSpec((tm, d_attn), lambda r: (r, 0)),
            pl.BlockSpec((tm, d_model), lambda r: (r, 0)),
            pl.BlockSpec((d_model, d_model), lambda r: (0, 0)),
            pl.BlockSpec((1, d_model), lambda r: (0, 0)),
            pl.BlockSpec((1, d_model), lambda r: (0, 0)),
        ],
        out_specs=pl.BlockSpec((tm, d_model), lambda r: (r, 0)),
        out_shape=jax.ShapeDtypeStruct((rows, d_model), h.dtype),
        compiler_params=pltpu.CompilerParams(
            dimension_semantics=("arbitrary",), vmem_limit_bytes=VMEM_LIMIT),
        name="out_proj_layernorm",
    )
    out = outp(y_pool, y_attn, x2, w_out.astype(_BF16),
               ln_gain.reshape(1, d_model), ln_bias.reshape(1, d_model))
    return out.reshape(b, s, d_model)


def kernel(x, positions, w_in, pool_w, pool_scale, w_out, ln_gain, ln_bias):
    depth = w_in.shape[0]
    alpha = (2.0 * depth) ** 0.25
    h = x
    for layer in range(depth):
        h = _layer(h, positions, w_in[layer], pool_w[layer], pool_scale[layer],
                   w_out[layer], ln_gain[layer], ln_bias[layer], alpha)
    return h
```

```python
import functools

import jax
import jax.numpy as jnp
from jax import lax
from jax.experimental import pallas as pl
from jax.experimental.pallas import tpu as pltpu

POOL_WINDOWS = (2, 4, 8, 16)
HEAD_DIM = 64
ROT_DIM = HEAD_DIM // 4
ROT_HALF = ROT_DIM // 2
ROPE_THETA = 500000.0
MOBA_BLOCK = 256
MOBA_TOPK = 3
LN_EPS = 1e-5

LANES = 128
POOL_HALO = 16
ROW_TILE = 512
MASK_NEG = -1e30
ONES_ROWS = 16
LOG2_E = 1.4426950408889634
VMEM_LIMIT = 48 * 1024 * 1024

_F32 = jnp.float32
_BF16 = jnp.bfloat16


def _silu(x):
    return x * (1.0 / (1.0 + jnp.exp(-x)))


def _dot(a, b):
    return jnp.dot(a, b, preferred_element_type=_F32)


def _dot_nt(a, b):
    return lax.dot_general(a, b, (((1,), (1,)), ((), ())), preferred_element_type=_F32)


def _rotate_heads(t, cos, sin, n_heads):
    pieces = []
    for h in range(n_heads):
        base = h * HEAD_DIM
        x1 = t[base:base + ROT_HALF]
        x2 = t[base + ROT_HALF:base + ROT_DIM]
        pieces.append(x1 * cos - x2 * sin)
        pieces.append(x2 * cos + x1 * sin)
        pieces.append(t[base + ROT_DIM:base + HEAD_DIM])
    return jnp.concatenate(pieces, axis=0)


def _proj_kernel(x_ref, pos_ref, freq_ref, wstd_ref, wt_ref, poolw_ref, pscale_ref,
                 ypool_ref, qt_ref, k_ref, vt_ref, km_ref, ga_ref, ubuf_ref,
                 *, d_pool, d_attn, n_heads, tm):
    s_idx = pl.program_id(1)
    xb = x_ref[...].astype(_BF16)

    std = _dot(xb, wstd_ref[...])
    tr = _dot_nt(wt_ref[...], xb)

    @pl.when(s_idx == 0)
    def _():
        ubuf_ref[0:POOL_HALO, :] = jnp.zeros((POOL_HALO, d_pool), _F32)

    ubuf_ref[POOL_HALO:POOL_HALO + tm, :] = std[:, 0:d_pool]
    group = d_pool // len(POOL_WINDOWS)
    t_glob = s_idx * tm + lax.broadcasted_iota(jnp.int32, (tm, group), 0)
    for g, w in enumerate(POOL_WINDOWS):
        cols = slice(g * group, (g + 1) * group)
        u_g = ubuf_ref[POOL_HALO:POOL_HALO + tm, cols]
        tsum = u_g
        for s in range(1, w):
            tsum = tsum + ubuf_ref[POOL_HALO - s:POOL_HALO - s + tm, cols]
        count = jnp.minimum(t_glob + 1, w).astype(_F32)
        d = tsum / count - u_g
        y = _dot(d.astype(_BF16), poolw_ref[g])
        gate = _silu(std[:, d_pool + g * group:d_pool + (g + 1) * group])
        ypool_ref[:, cols] = (y * pscale_ref[:, cols] * gate).astype(_BF16)
    ubuf_ref[0:POOL_HALO, :] = ubuf_ref[tm:tm + POOL_HALO, :]

    ga_ref[...] = _silu(std[:, 2 * d_pool:2 * d_pool + d_attn]).astype(_BF16)

    ang = freq_ref[...] * pos_ref[0].astype(_F32)
    cos = jnp.cos(ang)
    sin = jnp.sin(ang)

    scale = HEAD_DIM ** -0.5 * LOG2_E
    qt = _rotate_heads(tr[0:d_attn], cos, sin, n_heads) * scale
    qt = qt.astype(_BF16)
    for r in range(tm // MOBA_BLOCK):
        qt_ref[0, r] = qt[:, r * MOBA_BLOCK:(r + 1) * MOBA_BLOCK]

    kt = _rotate_heads(tr[d_attn:2 * d_attn], cos, sin, n_heads)
    for c in range(d_attn // LANES):
        kc = kt[c * LANES:(c + 1) * LANES].T
        k_ref[:, c * LANES:(c + 1) * LANES] = kc.astype(_BF16)
        for r in range(tm // MOBA_BLOCK):
            blk_sum = jnp.sum(kc[r * MOBA_BLOCK:(r + 1) * MOBA_BLOCK], axis=0, keepdims=True)
            km_ref[r, :, c * LANES:(c + 1) * LANES] = blk_sum * (1.0 / MOBA_BLOCK)

    vt = tr[2 * d_attn:3 * d_attn].astype(_BF16)
    for r in range(tm // MOBA_BLOCK):
        vt_ref[0, r] = vt[:, r * MOBA_BLOCK:(r + 1) * MOBA_BLOCK]


def _attn_kernel(qt_ref, k_ref, vt_ref, km_ref, ga_ref, o_ref,
                 qm_ref, bias_ref, m_ref, acc_ref, *, n_blocks, n_heads):
    i = pl.program_id(1)
    blk = MOBA_BLOCK
    pair = 2 * HEAD_DIM

    zeros_half = jnp.zeros((HEAD_DIM, blk), _BF16)
    for h in range(n_heads):
        base = (h // 2) * pair
        if h % 2 == 0:
            qm_ref[h] = jnp.concatenate([qt_ref[0, 0, base:base + HEAD_DIM, :], zeros_half], axis=0)
        else:
            qm_ref[h] = jnp.concatenate([zeros_half, qt_ref[0, 0, base + HEAD_DIM:base + pair, :]], axis=0)

    blk_id = lax.broadcasted_iota(jnp.int32, (n_blocks, blk), 0)
    valid = blk_id < i
    for h in range(n_heads):
        base = (h // 2) * pair
        km = km_ref[0, :, base:base + pair]
        km_hi = km.astype(_BF16)
        km_lo = (km - km_hi.astype(_F32)).astype(_BF16)
        qm = qm_ref[h]
        g = _dot(km_hi, qm) + _dot(km_lo, qm)
        g = jnp.where(valid, g, -jnp.inf)
        picked = jnp.zeros((n_blocks, blk), _F32)
        for _ in range(MOBA_TOPK):
            g_max = jnp.max(g, axis=0, keepdims=True)
            first = jnp.min(jnp.where(g == g_max, blk_id, n_blocks), axis=0, keepdims=True)
            hit = blk_id == first
            picked = jnp.where(hit, 1.0, picked)
            g = jnp.where(hit, -jnp.inf, g)
        bias = jnp.where(valid, jnp.where(picked > 0.0, 0.0, MASK_NEG), MASK_NEG)
        for j in range(n_blocks):
            bias_ref[h, j] = bias[j:j + 1, :]

    ones_rows = jnp.ones((ONES_ROWS, blk), _BF16)
    key_pos = lax.broadcasted_iota(jnp.int32, (blk, blk), 0)
    qry_pos = lax.broadcasted_iota(jnp.int32, (blk, blk), 1)
    causal = key_pos <= qry_pos

    def block_step(j, own):
        row0 = pl.multiple_of(j * blk, blk)
        k_pairs = [k_ref[pl.ds(row0, blk), p * pair:(p + 1) * pair] for p in range(n_heads // 2)]
        scores = [_dot(k_pairs[h // 2], qm_ref[h]) for h in range(n_heads)]
        for h in range(n_heads):
            s = scores[h]
            if own:
                s = jnp.where(causal, s, MASK_NEG)
                m_new = jnp.max(s, axis=0, keepdims=True)
            else:
                b = bias_ref[h, j]
                m_old = m_ref[h]
                m_new = jnp.maximum(m_old, jnp.max(s, axis=0, keepdims=True) + b)
            p = jnp.exp2(s - m_new).astype(_BF16)
            vt = jnp.concatenate([vt_ref[0, j, h * HEAD_DIM:(h + 1) * HEAD_DIM, :], ones_rows], axis=0)
            pv = _dot(vt, p)
            if own:
                acc_ref[h] = pv
            else:
                acc_ref[h] = jnp.exp2(m_old - m_new) * acc_ref[h] + jnp.where(b == 0.0, pv, 0.0)
            m_ref[h] = m_new

    block_step(i, True)

    def body(j, carry):
        block_step(j, False)
        return carry

    lax.fori_loop(0, i, body, 0)

    outs = []
    for h in range(n_heads):
        acc = acc_ref[h]
        outs.append(acc[0:HEAD_DIM] * (1.0 / acc[HEAD_DIM:HEAD_DIM + 1]))
    ot = jnp.concatenate(outs, axis=0)
    o_ref[...] = (ot.T * ga_ref[...].astype(_F32)).astype(_BF16)


def _out_kernel(yp_ref, ya_ref, x_ref, w_ref, gain_ref, bias_ref, o_ref, *, d_pool, alpha):
    out = _dot(yp_ref[...], w_ref[0:d_pool, :]) + _dot(ya_ref[...], w_ref[d_pool:, :])
    z = alpha * x_ref[...] + out
    mu = jnp.mean(z, axis=-1, keepdims=True)
    zc = z - mu
    var = jnp.mean(zc * zc, axis=-1, keepdims=True)
    o_ref[...] = zc * lax.rsqrt(var + LN_EPS) * gain_ref[...] + bias_ref[...]


def _layer(h, positions, w_in, pool_w, pool_scale, w_out, ln_gain, ln_bias, alpha):
    b, s, d_model = h.shape
    d_pool = pool_scale.shape[0]
    d_attn = d_model - d_pool
    n_heads = d_attn // HEAD_DIM
    n_blocks = s // MOBA_BLOCK
    tm = ROW_TILE
    n_tiles = s // tm
    rows = b * s
    blocks_per_tile = tm // MOBA_BLOCK
    assert s % tm == 0 and tm % MOBA_BLOCK == 0 and d_attn % LANES == 0
    assert w_in.shape == (d_model, 2 * d_pool + 4 * d_attn)

    c_q = 2 * d_pool
    c_g = 2 * d_pool + 3 * d_attn
    w_std = jnp.concatenate([w_in[:, :c_q], w_in[:, c_g:]], axis=1).astype(_BF16)
    w_t = w_in[:, c_q:c_g].T.astype(_BF16)
    freqs = ROPE_THETA ** (-jnp.arange(ROT_HALF, dtype=_F32) * 2.0 / ROT_DIM)
    freq_tab = jnp.broadcast_to(freqs[:, None], (ROT_HALF, tm))
    x2 = h.reshape(rows, d_model)
    pos3 = positions.reshape(b, 1, s)

    proj = pl.pallas_call(
        functools.partial(_proj_kernel, d_pool=d_pool, d_attn=d_attn, n_heads=n_heads, tm=tm),
        grid=(b, n_tiles),
        in_specs=[
            pl.BlockSpec((tm, d_model), lambda bi, si: (bi * n_tiles + si, 0)),
            pl.BlockSpec((1, 1, tm), lambda bi, si: (bi, 0, si)),
            pl.BlockSpec((ROT_HALF, tm), lambda bi, si: (0, 0)),
            pl.BlockSpec(w_std.shape, lambda bi, si: (0, 0)),
            pl.BlockSpec(w_t.shape, lambda bi, si: (0, 0)),
            pl.BlockSpec(pool_w.shape, lambda bi, si: (0, 0, 0)),
            pl.BlockSpec((1, d_pool), lambda bi, si: (0, 0)),
        ],
        out_specs=[
            pl.BlockSpec((tm, d_pool), lambda bi, si: (bi * n_tiles + si, 0)),
            pl.BlockSpec((1, blocks_per_tile, d_attn, MOBA_BLOCK), lambda bi, si: (bi, si, 0, 0)),
            pl.BlockSpec((tm, d_attn), lambda bi, si: (bi * n_tiles + si, 0)),
            pl.BlockSpec((1, blocks_per_tile, d_attn, MOBA_BLOCK), lambda bi, si: (bi, si, 0, 0)),
            pl.BlockSpec((blocks_per_tile, 1, d_attn), lambda bi, si: (bi * n_tiles + si, 0, 0)),
            pl.BlockSpec((tm, d_attn), lambda bi, si: (bi * n_tiles + si, 0)),
        ],
        out_shape=[
            jax.ShapeDtypeStruct((rows, d_pool), _BF16),
            jax.ShapeDtypeStruct((b, n_blocks, d_attn, MOBA_BLOCK), _BF16),
            jax.ShapeDtypeStruct((rows, d_attn), _BF16),
            jax.ShapeDtypeStruct((b, n_blocks, d_attn, MOBA_BLOCK), _BF16),
            jax.ShapeDtypeStruct((b * n_blocks, 1, d_attn), _F32),
            jax.ShapeDtypeStruct((rows, d_attn), _BF16),
        ],
        scratch_shapes=[pltpu.VMEM((tm + POOL_HALO, d_pool), _F32)],
        compiler_params=pltpu.CompilerParams(
            dimension_semantics=("arbitrary", "arbitrary"), vmem_limit_bytes=VMEM_LIMIT),
        name="proj_pool_rope",
    )
    y_pool, q_t, k_rot, v_t, k_mean, g_attn = proj(
        x2, pos3, freq_tab, w_std, w_t, pool_w.astype(_BF16), pool_scale.reshape(1, d_pool))

    k_mean = k_mean.reshape(b, n_blocks, d_attn)
    attn = pl.pallas_call(
        functools.partial(_attn_kernel, n_blocks=n_blocks, n_heads=n_heads),
        grid=(b, n_blocks),
        in_specs=[
            pl.BlockSpec((1, 1, d_attn, MOBA_BLOCK), lambda bi, i: (bi, i, 0, 0)),
            pl.BlockSpec((s, d_attn), lambda bi, i: (bi, 0)),
            pl.BlockSpec((1, n_blocks, d_attn, MOBA_BLOCK), lambda bi, i: (bi, 0, 0, 0)),
            pl.BlockSpec((1, n_blocks, d_attn), lambda bi, i: (bi, 0, 0)),
            pl.BlockSpec((MOBA_BLOCK, d_attn), lambda bi, i: (bi * n_blocks + i, 0)),
        ],
        out_specs=pl.BlockSpec((MOBA_BLOCK, d_attn), lambda bi, i: (bi * n_blocks + i, 0)),
        out_shape=jax.ShapeDtypeStruct((rows, d_attn), _BF16),
        scratch_shapes=[
            pltpu.VMEM((n_heads, 2 * HEAD_DIM, MOBA_BLOCK), _BF16),
            pltpu.VMEM((n_heads, n_blocks, 1, MOBA_BLOCK), _F32),
            pltpu.VMEM((n_heads, 1, MOBA_BLOCK), _F32),
            pltpu.VMEM((n_heads, HEAD_DIM + ONES_ROWS, MOBA_BLOCK), _F32),
        ],
        compiler_params=pltpu.CompilerParams(
            dimension_semantics=("arbitrary", "arbitrary"), vmem_limit_bytes=VMEM_LIMIT),
        name="moba_attention",
    )
    y_attn = attn(q_t, k_rot, v_t, k_mean, g_attn)

    outp = pl.pallas_call(
        functools.partial(_out_kernel, d_pool=d_pool, alpha=alpha),
        grid=(rows // tm,),
        in_specs=[
            pl.BlockSpec((tm, d_pool), lambda r: (r, 0)),
            pl.BlockSpec((tm, d_attn), lambda r: (r, 0)),
            pl.BlockSpec((tm, d_model), lambda r: (r, 0)),
            pl.BlockSpec((d_model, d_model), lambda r: (0, 0)),
            pl.BlockSpec((1, d_model), lambda r: (0, 0)),
            pl.BlockSpec((1, d_model), lambda r: (0, 0)),
        ],
        out_specs=pl.BlockSpec((tm, d_model), lambda r: (r, 0)),
        out_shape=jax.ShapeDtypeStruct((rows, d_model), h.dtype),
        compiler_params=pltpu.CompilerParams(
            dimension_semantics=("arbitrary",), vmem_limit_bytes=VMEM_LIMIT),
        name="out_proj_layernorm",
    )
    out = outp(y_pool, y_attn, x2, w_out.astype(_BF16),
               ln_gain.reshape(1, d_model), ln_bias.reshape(1, d_model))
    return out.reshape(b, s, d_model)


def kernel(x, positions, w_in, pool_w, pool_scale, w_out, ln_gain, ln_bias):
    depth = w_in.shape[0]
    alpha = (2.0 * depth) ** 0.25
    h = x
    for layer in range(depth):
        h = _layer(h, positions, w_in[layer], pool_w[layer], pool_scale[layer],
                   w_out[layer], ln_gain[layer], ln_bias[layer], alpha)
    return h
```

```python
import functools

import jax
import jax.numpy as jnp
from jax import lax
from jax.experimental import pallas as pl
from jax.experimental.pallas import tpu as pltpu

POOL_WINDOWS = (2, 4, 8, 16)
HEAD_DIM = 64
ROT_DIM = HEAD_DIM // 4
ROT_HALF = ROT_DIM // 2
ROPE_THETA = 500000.0
MOBA_BLOCK = 256
MOBA_TOPK = 3
LN_EPS = 1e-5

LANES = 128
POOL_HALO = 16
ROW_TILE = 512
MASK_NEG = -1e30
ONES_ROWS = 16
LOG2_E = 1.4426950408889634
BLOCKS_PER_ITER = 4
VMEM_LIMIT = 48 * 1024 * 1024

_F32 = jnp.float32
_BF16 = jnp.bfloat16


def _silu(x):
    return x * (1.0 / (1.0 + jnp.exp(-x)))


def _dot(a, b):
    return jnp.dot(a, b, preferred_element_type=_F32)


def _dot_nt(a, b):
    return lax.dot_general(a, b, (((1,), (1,)), ((), ())), preferred_element_type=_F32)


def _rotate_heads(t, cos, sin, n_heads):
    pieces = []
    for h in range(n_heads):
        base = h * HEAD_DIM
        x1 = t[base:base + ROT_HALF]
        x2 = t[base + ROT_HALF:base + ROT_DIM]
        pieces.append(x1 * cos - x2 * sin)
        pieces.append(x2 * cos + x1 * sin)
        pieces.append(t[base + ROT_DIM:base + HEAD_DIM])
    return jnp.concatenate(pieces, axis=0)


def _proj_kernel(x_ref, pos_ref, freq_ref, wstd_ref, wt_ref, poolw_ref, pscale_ref,
                 ypool_ref, qt_ref, k_ref, vt_ref, km_ref, ga_ref, ubuf_ref,
                 *, d_pool, d_attn, n_heads, tm):
    s_idx = pl.program_id(1)

    @pl.when((pl.program_id(0) == 0) & (s_idx == 0))
    def _():
        ubuf_ref[0:POOL_HALO, :] = jnp.zeros((POOL_HALO, d_pool), _F32)

    xb = x_ref[...].astype(_BF16)

    std = _dot(xb, wstd_ref[...])
    tr = _dot_nt(wt_ref[...], xb)

    ubuf_ref[POOL_HALO:POOL_HALO + tm, :] = std[:, 0:d_pool]
    group = d_pool // len(POOL_WINDOWS)
    t_glob = s_idx * tm + lax.broadcasted_iota(jnp.int32, (tm, group), 0)
    for g, w in enumerate(POOL_WINDOWS):
        cols = slice(g * group, (g + 1) * group)
        u_g = ubuf_ref[POOL_HALO:POOL_HALO + tm, cols]
        tsum = u_g
        for s in range(1, w):
            tsum = tsum + ubuf_ref[POOL_HALO - s:POOL_HALO - s + tm, cols]
        count = jnp.minimum(t_glob + 1, w).astype(_F32)
        d = tsum / count - u_g
        y = _dot(d.astype(_BF16), poolw_ref[g])
        gate = _silu(std[:, d_pool + g * group:d_pool + (g + 1) * group])
        ypool_ref[:, cols] = (y * pscale_ref[:, cols] * gate).astype(_BF16)
    last_tile = s_idx == pl.num_programs(1) - 1
    ubuf_ref[0:POOL_HALO, :] = jnp.where(last_tile, 0.0, ubuf_ref[tm:tm + POOL_HALO, :])

    ga_ref[...] = _silu(std[:, 2 * d_pool:2 * d_pool + d_attn]).astype(_BF16)

    ang = freq_ref[...] * pos_ref[0].astype(_F32)
    cos = jnp.cos(ang)
    sin = jnp.sin(ang)

    scale = HEAD_DIM ** -0.5 * LOG2_E
    qt = _rotate_heads(tr[0:d_attn], cos, sin, n_heads) * scale
    qt = qt.astype(_BF16)
    for r in range(tm // MOBA_BLOCK):
        qt_ref[0, r] = qt[:, r * MOBA_BLOCK:(r + 1) * MOBA_BLOCK]

    kt = _rotate_heads(tr[d_attn:2 * d_attn], cos, sin, n_heads)
    for c in range(d_attn // LANES):
        kc = kt[c * LANES:(c + 1) * LANES].T
        k_ref[:, c * LANES:(c + 1) * LANES] = kc.astype(_BF16)
        for r in range(tm // MOBA_BLOCK):
            blk_sum = jnp.sum(kc[r * MOBA_BLOCK:(r + 1) * MOBA_BLOCK], axis=0, keepdims=True)
            km_ref[r, :, c * LANES:(c + 1) * LANES] = blk_sum * (1.0 / MOBA_BLOCK)

    vt = tr[2 * d_attn:3 * d_attn].astype(_BF16)
    for r in range(tm // MOBA_BLOCK):
        vt_ref[0, r] = vt[:, r * MOBA_BLOCK:(r + 1) * MOBA_BLOCK]


def _attn_kernel(qt_ref, k_ref, vt_ref, km_ref, ga_ref, o_ref,
                 qm_ref, bias_ref, m_ref, acc_ref, s_ref, *, n_blocks, n_heads):
    i = pl.program_id(1)
    blk = MOBA_BLOCK
    pair = 2 * HEAD_DIM

    zeros_half = jnp.zeros((HEAD_DIM, blk), _BF16)
    for h in range(n_heads):
        base = (h // 2) * pair
        if h % 2 == 0:
            qm_ref[h] = jnp.concatenate([qt_ref[0, 0, base:base + HEAD_DIM, :], zeros_half], axis=0)
        else:
            qm_ref[h] = jnp.concatenate([zeros_half, qt_ref[0, 0, base + HEAD_DIM:base + pair, :]], axis=0)

    blk_id = lax.broadcasted_iota(jnp.int32, (n_blocks, blk), 0)
    valid = blk_id < i
    for h in range(n_heads):
        base = (h // 2) * pair
        km = km_ref[0, :, base:base + pair]
        km_hi = km.astype(_BF16)
        km_lo = (km - km_hi.astype(_F32)).astype(_BF16)
        qm = qm_ref[h]
        g = _dot(km_hi, qm) + _dot(km_lo, qm)
        g = jnp.where(valid, g, -jnp.inf)
        picked = jnp.zeros((n_blocks, blk), _F32)
        for _ in range(MOBA_TOPK):
            g_max = jnp.max(g, axis=0, keepdims=True)
            first = jnp.min(jnp.where(g == g_max, blk_id, n_blocks), axis=0, keepdims=True)
            hit = blk_id == first
            picked = jnp.where(hit, 1.0, picked)
            g = jnp.where(hit, -jnp.inf, g)
        bias = jnp.where(valid, jnp.where(picked > 0.0, 0.0, MASK_NEG), MASK_NEG)
        for j in range(n_blocks):
            bias_ref[h, j] = bias[j:j + 1, :]

    ones_rows = jnp.ones((ONES_ROWS, blk), _BF16)
    key_pos = lax.broadcasted_iota(jnp.int32, (blk, blk), 0)
    qry_pos = lax.broadcasted_iota(jnp.int32, (blk, blk), 1)
    causal = key_pos <= qry_pos

    def blocks_step(js, own):
        k_pairs = []
        for j in js:
            row0 = pl.multiple_of(j * blk, blk)
            k_pairs.append([k_ref[pl.ds(row0, blk), p * pair:(p + 1) * pair] for p in range(n_heads // 2)])
        for h in range(n_heads):
            for t in range(len(js)):
                s = _dot(k_pairs[t][h // 2], qm_ref[h])
                s_ref[t, h] = jnp.where(causal, s, MASK_NEG) if own else s
        for h in range(n_heads):
            if own:
                m_new = jnp.max(s_ref[0, h], axis=0, keepdims=True)
            else:
                bs = [bias_ref[h, j] for j in js]
                m_old = m_ref[h]
                m_new = m_old
                for t, b in enumerate(bs):
                    m_new = jnp.maximum(m_new, jnp.max(s_ref[t, h], axis=0, keepdims=True) + b)
            pv = None
            for t, j in enumerate(js):
                p = jnp.exp2(s_ref[t, h] - m_new).astype(_BF16)
                vt = jnp.concatenate([vt_ref[0, j, h * HEAD_DIM:(h + 1) * HEAD_DIM, :], ones_rows], axis=0)
                c = _dot(vt, p)
                if not own:
                    c = jnp.where(bs[t] == 0.0, c, 0.0)
                pv = c if pv is None else pv + c
            if own:
                acc_ref[h] = pv
            else:
                acc_ref[h] = jnp.exp2(m_old - m_new) * acc_ref[h] + pv
            m_ref[h] = m_new

    blocks_step([i], True)

    def body(u, carry):
        blocks_step([BLOCKS_PER_ITER * u + t for t in range(BLOCKS_PER_ITER)], False)
        return carry

    n_full = i // BLOCKS_PER_ITER
    lax.fori_loop(0, n_full, body, 0)
    for r in range(1, BLOCKS_PER_ITER):
        @pl.when(i - n_full * BLOCKS_PER_ITER == r)
        def _(r=r):
            blocks_step([n_full * BLOCKS_PER_ITER + t for t in range(r)], False)

    outs = []
    for h in range(n_heads):
        acc = acc_ref[h]
        outs.append(acc[0:HEAD_DIM] * (1.0 / acc[HEAD_DIM:HEAD_DIM + 1]))
    ot = jnp.concatenate(outs, axis=0)
    o_ref[...] = (ot.T * ga_ref[...].astype(_F32)).astype(_BF16)


def _out_kernel(yp_ref, ya_ref, x_ref, w_ref, gain_ref, bias_ref, o_ref, *, d_pool, alpha):
    out = _dot(yp_ref[...], w_ref[0:d_pool, :]) + _dot(ya_ref[...], w_ref[d_pool:, :])
    z = alpha * x_ref[...] + out
    mu = jnp.mean(z, axis=-1, keepdims=True)
    zc = z - mu
    var = jnp.mean(zc * zc, axis=-1, keepdims=True)
    o_ref[...] = zc * lax.rsqrt(var + LN_EPS) * gain_ref[...] + bias_ref[...]


def _layer(h, positions, w_in, pool_w, pool_scale, w_out, ln_gain, ln_bias, alpha):
    b, s, d_model = h.shape
    d_pool = pool_scale.shape[0]
    d_attn = d_model - d_pool
    n_heads = d_attn // HEAD_DIM
    n_blocks = s // MOBA_BLOCK
    tm = ROW_TILE
    n_tiles = s // tm
    rows = b * s
    blocks_per_tile = tm // MOBA_BLOCK
    assert s % tm == 0 and tm % MOBA_BLOCK == 0 and d_attn % LANES == 0
    assert w_in.shape == (d_model, 2 * d_pool + 4 * d_attn)

    c_q = 2 * d_pool
    c_g = 2 * d_pool + 3 * d_attn
    w_std = jnp.concatenate([w_in[:, :c_q], w_in[:, c_g:]], axis=1).astype(_BF16)
    w_t = w_in[:, c_q:c_g].T.astype(_BF16)
    freqs = ROPE_THETA ** (-jnp.arange(ROT_HALF, dtype=_F32) * 2.0 / ROT_DIM)
    freq_tab = jnp.broadcast_to(freqs[:, None], (ROT_HALF, tm))
    x2 = h.reshape(rows, d_model)
    pos3 = positions.reshape(b, 1, s)

    proj = pl.pallas_call(
        functools.partial(_proj_kernel, d_pool=d_pool, d_attn=d_attn, n_heads=n_heads, tm=tm),
        grid=(b, n_tiles),
        in_specs=[
            pl.BlockSpec((tm, d_model), lambda bi, si: (bi * n_tiles + si, 0)),
            pl.BlockSpec((1, 1, tm), lambda bi, si: (bi, 0, si)),
            pl.BlockSpec((ROT_HALF, tm), lambda bi, si: (0, 0)),
            pl.BlockSpec(w_std.shape, lambda bi, si: (0, 0)),
            pl.BlockSpec(w_t.shape, lambda bi, si: (0, 0)),
            pl.BlockSpec(pool_w.shape, lambda bi, si: (0, 0, 0)),
            pl.BlockSpec((1, d_pool), lambda bi, si: (0, 0)),
        ],
        out_specs=[
            pl.BlockSpec((tm, d_pool), lambda bi, si: (bi * n_tiles + si, 0)),
            pl.BlockSpec((1, blocks_per_tile, d_attn, MOBA_BLOCK), lambda bi, si: (bi, si, 0, 0)),
            pl.BlockSpec((tm, d_attn), lambda bi, si: (bi * n_tiles + si, 0)),
            pl.BlockSpec((1, blocks_per_tile, d_attn, MOBA_BLOCK), lambda bi, si: (bi, si, 0, 0)),
            pl.BlockSpec((blocks_per_tile, 1, d_attn), lambda bi, si: (bi * n_tiles + si, 0, 0)),
            pl.BlockSpec((tm, d_attn), lambda bi, si: (bi * n_tiles + si, 0)),
        ],
        out_shape=[
            jax.ShapeDtypeStruct((rows, d_pool), _BF16),
            jax.ShapeDtypeStruct((b, n_blocks, d_attn, MOBA_BLOCK), _BF16),
            jax.ShapeDtypeStruct((rows, d_attn), _BF16),
            jax.ShapeDtypeStruct((b, n_blocks, d_attn, MOBA_BLOCK), _BF16),
            jax.ShapeDtypeStruct((b * n_blocks, 1, d_attn), _F32),
            jax.ShapeDtypeStruct((rows, d_attn), _BF16),
        ],
        scratch_shapes=[pltpu.VMEM((tm + POOL_HALO, d_pool), _F32)],
        compiler_params=pltpu.CompilerParams(
            dimension_semantics=("arbitrary", "arbitrary"), vmem_limit_bytes=VMEM_LIMIT),
        name="proj_pool_rope",
    )
    y_pool, q_t, k_rot, v_t, k_mean, g_attn = proj(
        x2, pos3, freq_tab, w_std, w_t, pool_w.astype(_BF16), pool_scale.reshape(1, d_pool))

    k_mean = k_mean.reshape(b, n_blocks, d_attn)
    attn = pl.pallas_call(
        functools.partial(_attn_kernel, n_blocks=n_blocks, n_heads=n_heads),
        grid=(b, n_blocks),
        in_specs=[
            pl.BlockSpec((1, 1, d_attn, MOBA_BLOCK), lambda bi, i: (bi, i, 0, 0)),
            pl.BlockSpec((s, d_attn), lambda bi, i: (bi, 0)),
            pl.BlockSpec((1, n_blocks, d_attn, MOBA_BLOCK), lambda bi, i: (bi, 0, 0, 0)),
            pl.BlockSpec((1, n_blocks, d_attn), lambda bi, i: (bi, 0, 0)),
            pl.BlockSpec((MOBA_BLOCK, d_attn), lambda bi, i: (bi * n_blocks + i, 0)),
        ],
        out_specs=pl.BlockSpec((MOBA_BLOCK, d_attn), lambda bi, i: (bi * n_blocks + i, 0)),
        out_shape=jax.ShapeDtypeStruct((rows, d_attn), _BF16),
        scratch_shapes=[
            pltpu.VMEM((n_heads, 2 * HEAD_DIM, MOBA_BLOCK), _BF16),
            pltpu.VMEM((n_heads, n_blocks, 1, MOBA_BLOCK), _F32),
            pltpu.VMEM((n_heads, 1, MOBA_BLOCK), _F32),
            pltpu.VMEM((n_heads, HEAD_DIM + ONES_ROWS, MOBA_BLOCK), _F32),
            pltpu.VMEM((BLOCKS_PER_ITER, n_heads, MOBA_BLOCK, MOBA_BLOCK), _F32),
        ],
        compiler_params=pltpu.CompilerParams(
            dimension_semantics=("arbitrary", "arbitrary"), vmem_limit_bytes=VMEM_LIMIT),
        name="moba_attention",
    )
    y_attn = attn(q_t, k_rot, v_t, k_mean, g_attn)

    outp = pl.pallas_call(
        functools.partial(_out_kernel, d_pool=d_pool, alpha=alpha),
        grid=(rows // tm,),
        in_specs=[
            pl.BlockSpec((tm, d_pool), lambda r: (r, 0)),
            pl.BlockSpec((tm, d_attn), lambda r: (r, 0)),
            pl.BlockSpec((tm, d_model), lambda r: (r, 0)),
            pl.BlockSpec((d_model, d_model), lambda r: (0, 0)),
            pl.BlockSpec((1, d_model), lambda r: (0, 0)),
            pl.BlockSpec((1, d_model), lambda r: (0, 0)),
        ],
        out_specs=pl.BlockSpec((tm, d_model), lambda r: (r, 0)),
        out_shape=jax.ShapeDtypeStruct((rows, d_model), h.dtype),
        compiler_params=pltpu.CompilerParams(
            dimension_semantics=("arbitrary",), vmem_limit_bytes=VMEM_LIMIT),
        name="out_proj_layernorm",
    )
    out = outp(y_pool, y_attn, x2, w_out.astype(_BF16),
               ln_gain.reshape(1, d_model), ln_bias.reshape(1, d_model))
    return out.reshape(b, s, d_model)


def kernel(x, positions, w_in, pool_w, pool_scale, w_out, ln_gain, ln_bias):
    depth = w_in.shape[0]
    alpha = (2.0 * depth) ** 0.25
    h = x
    for layer in range(depth):
        h = _layer(h, positions, w_in[layer], pool_w[layer], pool_scale[layer],
                   w_out[layer], ln_gain[layer], ln_bias[layer], alpha)
    return h
```

```python
import functools

import jax
import jax.numpy as jnp
from jax import lax
from jax.experimental import pallas as pl
from jax.experimental.pallas import tpu as pltpu

POOL_WINDOWS = (2, 4, 8, 16)
HEAD_DIM = 64
ROT_DIM = HEAD_DIM // 4
ROT_HALF = ROT_DIM // 2
ROPE_THETA = 500000.0
MOBA_BLOCK = 256
MOBA_TOPK = 3
LN_EPS = 1e-5

LANES = 128
POOL_HALO = 16
ROW_TILE = 512
MASK_NEG = -1e30
ONES_ROWS = 16
LOG2_E = 1.4426950408889634
OUT_ROW_CHUNKS = 2
BLOCKS_PER_ITER = 4
VMEM_LIMIT = 48 * 1024 * 1024

_F32 = jnp.float32
_BF16 = jnp.bfloat16


def _silu(x):
    return x * (1.0 / (1.0 + jnp.exp(-x)))


def _dot(a, b):
    return jnp.dot(a, b, preferred_element_type=_F32)


def _dot_nt(a, b):
    return lax.dot_general(a, b, (((1,), (1,)), ((), ())), preferred_element_type=_F32)


def _rotate_heads(t, cos, sin, n_heads):
    pieces = []
    for h in range(n_heads):
        base = h * HEAD_DIM
        x1 = t[base:base + ROT_HALF]
        x2 = t[base + ROT_HALF:base + ROT_DIM]
        pieces.append(x1 * cos - x2 * sin)
        pieces.append(x2 * cos + x1 * sin)
        pieces.append(t[base + ROT_DIM:base + HEAD_DIM])
    return jnp.concatenate(pieces, axis=0)


def _proj_kernel(x_ref, pos_ref, freq_ref, wstd_ref, wt_ref, poolw_ref, pscale_ref,
                 ypool_ref, qt_ref, k_ref, vt_ref, km_ref, ga_ref, ubuf_ref,
                 *, d_pool, d_attn, n_heads, tm):
    s_idx = pl.program_id(1)

    @pl.when((pl.program_id(0) == 0) & (s_idx == 0))
    def _():
        ubuf_ref[0:POOL_HALO, :] = jnp.zeros((POOL_HALO, d_pool), _F32)

    xb = x_ref[...].astype(_BF16)

    std = _dot(xb, wstd_ref[...])
    tr = _dot_nt(wt_ref[...], xb)

    ubuf_ref[POOL_HALO:POOL_HALO + tm, :] = std[:, 0:d_pool]
    group = d_pool // len(POOL_WINDOWS)
    t_glob = s_idx * tm + lax.broadcasted_iota(jnp.int32, (tm, group), 0)
    for g, w in enumerate(POOL_WINDOWS):
        cols = slice(g * group, (g + 1) * group)
        u_g = ubuf_ref[POOL_HALO:POOL_HALO + tm, cols]
        tsum = u_g
        for s in range(1, w):
            tsum = tsum + ubuf_ref[POOL_HALO - s:POOL_HALO - s + tm, cols]
        count = jnp.minimum(t_glob + 1, w).astype(_F32)
        d = tsum / count - u_g
        y = _dot(d.astype(_BF16), poolw_ref[g])
        gate = _silu(std[:, d_pool + g * group:d_pool + (g + 1) * group])
        ypool_ref[:, cols] = (y * pscale_ref[:, cols] * gate).astype(_BF16)
    last_tile = s_idx == pl.num_programs(1) - 1
    ubuf_ref[0:POOL_HALO, :] = jnp.where(last_tile, 0.0, ubuf_ref[tm:tm + POOL_HALO, :])

    ga_ref[...] = _silu(std[:, 2 * d_pool:2 * d_pool + d_attn]).astype(_BF16)

    ang = freq_ref[...] * pos_ref[0].astype(_F32)
    cos = jnp.cos(ang)
    sin = jnp.sin(ang)

    scale = HEAD_DIM ** -0.5 * LOG2_E
    qt = _rotate_heads(tr[0:d_attn], cos, sin, n_heads) * scale
    qt = qt.astype(_BF16)
    for r in range(tm // MOBA_BLOCK):
        qt_ref[0, r] = qt[:, r * MOBA_BLOCK:(r + 1) * MOBA_BLOCK]

    kt = _rotate_heads(tr[d_attn:2 * d_attn], cos, sin, n_heads)
    for c in range(d_attn // LANES):
        kc = kt[c * LANES:(c + 1) * LANES].T
        k_ref[:, c * LANES:(c + 1) * LANES] = kc.astype(_BF16)
        for r in range(tm // MOBA_BLOCK):
            blk_sum = jnp.sum(kc[r * MOBA_BLOCK:(r + 1) * MOBA_BLOCK], axis=0, keepdims=True)
            km_ref[r, :, c * LANES:(c + 1) * LANES] = blk_sum * (1.0 / MOBA_BLOCK)

    vt = tr[2 * d_attn:3 * d_attn].astype(_BF16)
    for r in range(tm // MOBA_BLOCK):
        vt_ref[0, r] = vt[:, r * MOBA_BLOCK:(r + 1) * MOBA_BLOCK]


def _attn_kernel(qt_ref, k_ref, vt_ref, km_ref, ga_ref, o_ref,
                 qm_ref, bias_ref, m_ref, acc_ref, s_ref, *, n_blocks, n_heads):
    i = pl.program_id(1)
    blk = MOBA_BLOCK
    pair = 2 * HEAD_DIM

    zeros_half = jnp.zeros((HEAD_DIM, blk), _BF16)
    for h in range(n_heads):
        base = (h // 2) * pair
        if h % 2 == 0:
            qm_ref[h] = jnp.concatenate([qt_ref[0, 0, base:base + HEAD_DIM, :], zeros_half], axis=0)
        else:
            qm_ref[h] = jnp.concatenate([zeros_half, qt_ref[0, 0, base + HEAD_DIM:base + pair, :]], axis=0)

    blk_id = lax.broadcasted_iota(jnp.int32, (n_blocks, blk), 0)
    valid = blk_id < i
    for h in range(n_heads):
        base = (h // 2) * pair
        km = km_ref[0, :, base:base + pair]
        km_hi = km.astype(_BF16)
        km_lo = (km - km_hi.astype(_F32)).astype(_BF16)
        qm = qm_ref[h]
        g = _dot(km_hi, qm) + _dot(km_lo, qm)
        g = jnp.where(valid, g, -jnp.inf)
        picked = jnp.zeros((n_blocks, blk), _F32)
        for _ in range(MOBA_TOPK):
            g_max = jnp.max(g, axis=0, keepdims=True)
            first = jnp.min(jnp.where(g == g_max, blk_id, n_blocks), axis=0, keepdims=True)
            hit = blk_id == first
            picked = jnp.where(hit, 1.0, picked)
            g = jnp.where(hit, -jnp.inf, g)
        bias = jnp.where(valid, jnp.where(picked > 0.0, 0.0, MASK_NEG), MASK_NEG)
        for j in range(n_blocks):
            bias_ref[h, j] = bias[j:j + 1, :]

    ones_rows = jnp.ones((ONES_ROWS, blk), _BF16)
    key_pos = lax.broadcasted_iota(jnp.int32, (blk, blk), 0)
    qry_pos = lax.broadcasted_iota(jnp.int32, (blk, blk), 1)
    causal = key_pos <= qry_pos

    def blocks_step(js, first):
        k_pairs = []
        for j in js:
            row0 = pl.multiple_of(j * blk, blk)
            k_pairs.append([k_ref[pl.ds(row0, blk), p * pair:(p + 1) * pair] for p in range(n_heads // 2)])
        for h in range(n_heads):
            for t in range(len(js)):
                s = _dot(k_pairs[t][h // 2], qm_ref[h])
                s_ref[t, h] = jnp.where(causal, s, MASK_NEG) if (first and t == 0) else s
        for h in range(n_heads):
            bs = [None if (first and t == 0) else bias_ref[h, j] for t, j in enumerate(js)]
            m_old = None if first else m_ref[h]
            m_new = m_old
            for t, b in enumerate(bs):
                cand = jnp.max(s_ref[t, h], axis=0, keepdims=True)
                if b is not None:
                    cand = cand + b
                m_new = cand if m_new is None else jnp.maximum(m_new, cand)
            pv = None
            for t, j in enumerate(js):
                p = jnp.exp2(s_ref[t, h] - m_new).astype(_BF16)
                vt = jnp.concatenate([vt_ref[0, j, h * HEAD_DIM:(h + 1) * HEAD_DIM, :], ones_rows], axis=0)
                c = _dot(vt, p)
                if bs[t] is not None:
                    c = jnp.where(bs[t] == 0.0, c, 0.0)
                pv = c if pv is None else pv + c
            if first:
                acc_ref[h] = pv
            else:
                acc_ref[h] = jnp.exp2(m_old - m_new) * acc_ref[h] + pv
            m_ref[h] = m_new

    n_lead = i % BLOCKS_PER_ITER
    for r in range(BLOCKS_PER_ITER):
        @pl.when(n_lead == r)
        def _(r=r):
            blocks_step([i] + list(range(r)), True)

    def body(u, carry):
        blocks_step([n_lead + BLOCKS_PER_ITER * u + t for t in range(BLOCKS_PER_ITER)], False)
        return carry

    lax.fori_loop(0, i // BLOCKS_PER_ITER, body, 0)

    outs = []
    for h in range(n_heads):
        acc = acc_ref[h]
        outs.append(acc[0:HEAD_DIM] * (1.0 / acc[HEAD_DIM:HEAD_DIM + 1]))
    ot = jnp.concatenate(outs, axis=0)
    o_ref[...] = (ot.T * ga_ref[...].astype(_F32)).astype(_BF16)


def _out_kernel(yp_ref, ya_ref, x_ref, w_ref, gain_ref, bias_ref, o_ref, *, d_pool, alpha, n_chunks):
    rows = x_ref.shape[0] // n_chunks
    for c in range(n_chunks):
        r = slice(c * rows, (c + 1) * rows)
        out = _dot(yp_ref[r, :], w_ref[0:d_pool, :]) + _dot(ya_ref[r, :], w_ref[d_pool:, :])
        z = alpha * x_ref[r, :] + out
        mu = jnp.mean(z, axis=-1, keepdims=True)
        zc = z - mu
        var = jnp.mean(zc * zc, axis=-1, keepdims=True)
        o_ref[r, :] = zc * lax.rsqrt(var + LN_EPS) * gain_ref[...] + bias_ref[...]


def _layer(h, positions, w_in, pool_w, pool_scale, w_out, ln_gain, ln_bias, alpha):
    b, s, d_model = h.shape
    d_pool = pool_scale.shape[0]
    d_attn = d_model - d_pool
    n_heads = d_attn // HEAD_DIM
    n_blocks = s // MOBA_BLOCK
    tm = ROW_TILE
    n_tiles = s // tm
    rows = b * s
    blocks_per_tile = tm // MOBA_BLOCK
    assert s % tm == 0 and tm % MOBA_BLOCK == 0 and d_attn % LANES == 0
    assert w_in.shape == (d_model, 2 * d_pool + 4 * d_attn)

    c_q = 2 * d_pool
    c_g = 2 * d_pool + 3 * d_attn
    w_std = jnp.concatenate([w_in[:, :c_q], w_in[:, c_g:]], axis=1).astype(_BF16)
    w_t = w_in[:, c_q:c_g].T.astype(_BF16)
    freqs = ROPE_THETA ** (-jnp.arange(ROT_HALF, dtype=_F32) * 2.0 / ROT_DIM)
    freq_tab = jnp.broadcast_to(freqs[:, None], (ROT_HALF, tm))
    x2 = h.reshape(rows, d_model)
    pos3 = positions.reshape(b, 1, s)

    proj = pl.pallas_call(
        functools.partial(_proj_kernel, d_pool=d_pool, d_attn=d_attn, n_heads=n_heads, tm=tm),
        grid=(b, n_tiles),
        in_specs=[
            pl.BlockSpec((tm, d_model), lambda bi, si: (bi * n_tiles + si, 0)),
            pl.BlockSpec((1, 1, tm), lambda bi, si: (bi, 0, si)),
            pl.BlockSpec((ROT_HALF, tm), lambda bi, si: (0, 0)),
            pl.BlockSpec(w_std.shape, lambda bi, si: (0, 0)),
            pl.BlockSpec(w_t.shape, lambda bi, si: (0, 0)),
            pl.BlockSpec(pool_w.shape, lambda bi, si: (0, 0, 0)),
            pl.BlockSpec((1, d_pool), lambda bi, si: (0, 0)),
        ],
        out_specs=[
            pl.BlockSpec((tm, d_pool), lambda bi, si: (bi * n_tiles + si, 0)),
            pl.BlockSpec((1, blocks_per_tile, d_attn, MOBA_BLOCK), lambda bi, si: (bi, si, 0, 0)),
            pl.BlockSpec((tm, d_attn), lambda bi, si: (bi * n_tiles + si, 0)),
            pl.BlockSpec((1, blocks_per_tile, d_attn, MOBA_BLOCK), lambda bi, si: (bi, si, 0, 0)),
            pl.BlockSpec((blocks_per_tile, 1, d_attn), lambda bi, si: (bi * n_tiles + si, 0, 0)),
            pl.BlockSpec((tm, d_attn), lambda bi, si: (bi * n_tiles + si, 0)),
        ],
        out_shape=[
            jax.ShapeDtypeStruct((rows, d_pool), _BF16),
            jax.ShapeDtypeStruct((b, n_blocks, d_attn, MOBA_BLOCK), _BF16),
            jax.ShapeDtypeStruct((rows, d_attn), _BF16),
            jax.ShapeDtypeStruct((b, n_blocks, d_attn, MOBA_BLOCK), _BF16),
            jax.ShapeDtypeStruct((b * n_blocks, 1, d_attn), _F32),
            jax.ShapeDtypeStruct((rows, d_attn), _BF16),
        ],
        scratch_shapes=[pltpu.VMEM((tm + POOL_HALO, d_pool), _F32)],
        compiler_params=pltpu.CompilerParams(
            dimension_semantics=("arbitrary", "arbitrary"), vmem_limit_bytes=VMEM_LIMIT),
        name="proj_pool_rope",
    )
    y_pool, q_t, k_rot, v_t, k_mean, g_attn = proj(
        x2, pos3, freq_tab, w_std, w_t, pool_w.astype(_BF16), pool_scale.reshape(1, d_pool))

    k_mean = k_mean.reshape(b, n_blocks, d_attn)
    attn = pl.pallas_call(
        functools.partial(_attn_kernel, n_blocks=n_blocks, n_heads=n_heads),
        grid=(b, n_blocks),
        in_specs=[
            pl.BlockSpec((1, 1, d_attn, MOBA_BLOCK), lambda bi, i: (bi, i, 0, 0)),
            pl.BlockSpec((s, d_attn), lambda bi, i: (bi, 0)),
            pl.BlockSpec((1, n_blocks, d_attn, MOBA_BLOCK), lambda bi, i: (bi, 0, 0, 0)),
            pl.BlockSpec((1, n_blocks, d_attn), lambda bi, i: (bi, 0, 0)),
            pl.BlockSpec((MOBA_BLOCK, d_attn), lambda bi, i: (bi * n_blocks + i, 0)),
        ],
        out_specs=pl.BlockSpec((MOBA_BLOCK, d_attn), lambda bi, i: (bi * n_blocks + i, 0)),
        out_shape=jax.ShapeDtypeStruct((rows, d_attn), _BF16),
        scratch_shapes=[
            pltpu.VMEM((n_heads, 2 * HEAD_DIM, MOBA_BLOCK), _BF16),
            pltpu.VMEM((n_heads, n_blocks, 1, MOBA_BLOCK), _F32),
            pltpu.VMEM((n_heads, 1, MOBA_BLOCK), _F32),
            pltpu.VMEM((n_heads, HEAD_DIM + ONES_ROWS, MOBA_BLOCK), _F32),
            pltpu.VMEM((BLOCKS_PER_ITER, n_heads, MOBA_BLOCK, MOBA_BLOCK), _F32),
        ],
        compiler_params=pltpu.CompilerParams(
            dimension_semantics=("arbitrary", "arbitrary"), vmem_limit_bytes=VMEM_LIMIT),
        name="moba_attention",
    )
    y_attn = attn(q_t, k_rot, v_t, k_mean, g_attn)

    outp = pl.pallas_call(
        functools.partial(_out_kernel, d_pool=d_pool, alpha=alpha, n_chunks=OUT_ROW_CHUNKS),
        grid=(rows // tm,),
        in_specs=[
            pl.BlockSpec((tm, d_pool), lambda r: (r, 0)),
            pl.BlockSpec((tm, d_attn), lambda r: (r, 0)),
            pl.BlockSpec((tm, d_model), lambda r: (r, 0)),
            pl.BlockSpec((d_model, d_model), lambda r: (0, 0)),
            pl.BlockSpec((1, d_model), lambda r: (0, 0)),
            pl.BlockSpec((1, d_model), lambda r: (0, 0)),
        ],
        out_specs=pl.BlockSpec((tm, d_model), lambda r: (r, 0)),
        out_shape=jax.ShapeDtypeStruct((rows, d_model), h.dtype),
        compiler_params=pltpu.CompilerParams(
            dimension_semantics=("arbitrary",), vmem_limit_bytes=VMEM_LIMIT),
        name="out_proj_layernorm",
    )
    out = outp(y_pool, y_attn, x2, w_out.astype(_BF16),
               ln_gain.reshape(1, d_model), ln_bias.reshape(1, d_model))
    return out.reshape(b, s, d_model)


def kernel(x, positions, w_in, pool_w, pool_scale, w_out, ln_gain, ln_bias):
    depth = w_in.shape[0]
    alpha = (2.0 * depth) ** 0.25
    h = x
    for layer in range(depth):
        h = _layer(h, positions, w_in[layer], pool_w[layer], pool_scale[layer],
                   w_out[layer], ln_gain[layer], ln_bias[layer], alpha)
    return h
```

```python
import functools

import jax
import jax.numpy as jnp
from jax import lax
from jax.experimental import pallas as pl
from jax.experimental.pallas import tpu as pltpu

POOL_WINDOWS = (2, 4, 8, 16)
HEAD_DIM = 64
ROT_DIM = HEAD_DIM // 4
ROT_HALF = ROT_DIM // 2
ROPE_THETA = 500000.0
MOBA_BLOCK = 256
MOBA_TOPK = 3
LN_EPS = 1e-5

LANES = 128
POOL_HALO = 16
ROW_TILE = 1024
MASK_NEG = -1e30
ONES_ROWS = 16
LOG2_E = 1.4426950408889634
BLOCKS_PER_ITER = 4
QK_LOOKAHEAD = {1: 8, 2: 3, 3: 3, 4: 2}
OUT_ROW_TILE = 1024
OUT_ROW_CHUNKS = 4
VMEM_LIMIT = 48 * 1024 * 1024

_F32 = jnp.float32
_BF16 = jnp.bfloat16


def _silu(x):
    return x * (1.0 / (1.0 + jnp.exp(-x)))


def _dot(a, b):
    return jnp.dot(a, b, preferred_element_type=_F32)


def _dot_nt(a, b):
    return lax.dot_general(a, b, (((1,), (1,)), ((), ())), preferred_element_type=_F32)


def _rotate_heads(t, cos, sin, n_heads):
    pieces = []
    for h in range(n_heads):
        base = h * HEAD_DIM
        x1 = t[base:base + ROT_HALF]
        x2 = t[base + ROT_HALF:base + ROT_DIM]
        pieces.append(x1 * cos - x2 * sin)
        pieces.append(x2 * cos + x1 * sin)
        pieces.append(t[base + ROT_DIM:base + HEAD_DIM])
    return jnp.concatenate(pieces, axis=0)


def _proj_kernel(x_ref, pos_ref, freq_ref, wstd_ref, wt_ref, poolw_ref, pscale_ref,
                 ypool_ref, qt_ref, k_ref, vt_ref, km_ref, ga_ref, ubuf_ref,
                 *, d_pool, d_attn, n_heads, tm):
    s_idx = pl.program_id(1)

    @pl.when((pl.program_id(0) == 0) & (s_idx == 0))
    def _():
        ubuf_ref[0:POOL_HALO, :] = jnp.zeros((POOL_HALO, d_pool), _F32)

    xb = x_ref[...].astype(_BF16)

    std = _dot(xb, wstd_ref[...])
    tr = _dot_nt(wt_ref[...], xb)

    ubuf_ref[POOL_HALO:POOL_HALO + tm, :] = std[:, 0:d_pool]
    group = d_pool // len(POOL_WINDOWS)
    t_glob = s_idx * tm + lax.broadcasted_iota(jnp.int32, (tm, group), 0)
    for g, w in enumerate(POOL_WINDOWS):
        cols = slice(g * group, (g + 1) * group)
        u_g = ubuf_ref[POOL_HALO:POOL_HALO + tm, cols]
        tsum = u_g
        for s in range(1, w):
            tsum = tsum + ubuf_ref[POOL_HALO - s:POOL_HALO - s + tm, cols]
        count = jnp.minimum(t_glob + 1, w).astype(_F32)
        d = tsum / count - u_g
        y = _dot(d.astype(_BF16), poolw_ref[g])
        gate = _silu(std[:, d_pool + g * group:d_pool + (g + 1) * group])
        ypool_ref[:, cols] = (y * pscale_ref[:, cols] * gate).astype(_BF16)
    last_tile = s_idx == pl.num_programs(1) - 1
    ubuf_ref[0:POOL_HALO, :] = jnp.where(last_tile, 0.0, ubuf_ref[tm:tm + POOL_HALO, :])

    ga_ref[...] = _silu(std[:, 2 * d_pool:2 * d_pool + d_attn]).astype(_BF16)

    ang = freq_ref[...] * pos_ref[0].astype(_F32)
    cos = jnp.cos(ang)
    sin = jnp.sin(ang)

    scale = HEAD_DIM ** -0.5 * LOG2_E
    qt = _rotate_heads(tr[0:d_attn], cos, sin, n_heads) * scale
    qt = qt.astype(_BF16)
    for r in range(tm // MOBA_BLOCK):
        qt_ref[0, r] = qt[:, r * MOBA_BLOCK:(r + 1) * MOBA_BLOCK]

    kt = _rotate_heads(tr[d_attn:2 * d_attn], cos, sin, n_heads)
    for c in range(d_attn // LANES):
        kc = kt[c * LANES:(c + 1) * LANES].T
        k_ref[:, c * LANES:(c + 1) * LANES] = kc.astype(_BF16)
        for r in range(tm // MOBA_BLOCK):
            blk_sum = jnp.sum(kc[r * MOBA_BLOCK:(r + 1) * MOBA_BLOCK], axis=0, keepdims=True)
            km_ref[r, :, c * LANES:(c + 1) * LANES] = blk_sum * (1.0 / MOBA_BLOCK)

    vt = tr[2 * d_attn:3 * d_attn].astype(_BF16)
    for r in range(tm // MOBA_BLOCK):
        vt_ref[0, r] = vt[:, r * MOBA_BLOCK:(r + 1) * MOBA_BLOCK]


def _attn_kernel(qt_ref, k_ref, vt_ref, km_ref, ga_ref, o_ref,
                 qm_ref, bias_ref, m_ref, acc_ref, s_ref, *, n_blocks, n_heads):
    i = pl.program_id(1)
    blk = MOBA_BLOCK
    pair = 2 * HEAD_DIM

    zeros_half = jnp.zeros((HEAD_DIM, blk), _BF16)
    for h in range(n_heads):
        base = (h // 2) * pair
        if h % 2 == 0:
            qm_ref[h] = jnp.concatenate([qt_ref[0, 0, base:base + HEAD_DIM, :], zeros_half], axis=0)
        else:
            qm_ref[h] = jnp.concatenate([zeros_half, qt_ref[0, 0, base + HEAD_DIM:base + pair, :]], axis=0)

    blk_id = lax.broadcasted_iota(jnp.int32, (n_blocks, blk), 0)
    valid = blk_id < i
    for h in range(n_heads):
        base = (h // 2) * pair
        km = km_ref[0, :, base:base + pair]
        km_hi = km.astype(_BF16)
        km_lo = (km - km_hi.astype(_F32)).astype(_BF16)
        qm = qm_ref[h]
        g = _dot(km_hi, qm) + _dot(km_lo, qm)
        g = jnp.where(valid, g, -jnp.inf)
        picked = jnp.zeros((n_blocks, blk), _F32)
        for _ in range(MOBA_TOPK):
            g_max = jnp.max(g, axis=0, keepdims=True)
            first = jnp.min(jnp.where(g == g_max, blk_id, n_blocks), axis=0, keepdims=True)
            hit = blk_id == first
            picked = jnp.where(hit, 1.0, picked)
            g = jnp.where(hit, -jnp.inf, g)
        bias = jnp.where(valid, jnp.where(picked > 0.0, 0.0, MASK_NEG), MASK_NEG)
        for j in range(n_blocks):
            bias_ref[h, j] = bias[j:j + 1, :]

    ones_rows = jnp.ones((ONES_ROWS, blk), _BF16)
    key_pos = lax.broadcasted_iota(jnp.int32, (blk, blk), 0)
    qry_pos = lax.broadcasted_iota(jnp.int32, (blk, blk), 1)
    causal = key_pos <= qry_pos

    def blocks_step(js, first):
        k_pairs = []
        for j in js:
            row0 = pl.multiple_of(j * blk, blk)
            k_pairs.append([k_ref[pl.ds(row0, blk), p * pair:(p + 1) * pair] for p in range(n_heads // 2)])
        def issue_scores(h):
            for t in range(len(js)):
                s = _dot(k_pairs[t][h // 2], qm_ref[h])
                s_ref[t, h] = jnp.where(causal, s, MASK_NEG) if (first and t == 0) else s

        ahead = min(QK_LOOKAHEAD[len(js)], n_heads)
        for h in range(ahead):
            issue_scores(h)
        for h in range(n_heads):
            if h + ahead < n_heads:
                issue_scores(h + ahead)
            bs = [None if (first and t == 0) else bias_ref[h, j] for t, j in enumerate(js)]
            m_old = None if first else m_ref[h]
            m_new = m_old
            for t, b in enumerate(bs):
                cand = jnp.max(s_ref[t, h], axis=0, keepdims=True)
                if b is not None:
                    cand = cand + b
                m_new = cand if m_new is None else jnp.maximum(m_new, cand)
            pv = None
            for t, j in enumerate(js):
                p = jnp.exp2(s_ref[t, h] - m_new).astype(_BF16)
                vt = jnp.concatenate([vt_ref[0, j, h * HEAD_DIM:(h + 1) * HEAD_DIM, :], ones_rows], axis=0)
                c = _dot(vt, p)
                if bs[t] is not None:
                    c = jnp.where(bs[t] == 0.0, c, 0.0)
                pv = c if pv is None else pv + c
            if first:
                acc_ref[h] = pv
            else:
                acc_ref[h] = jnp.exp2(m_old - m_new) * acc_ref[h] + pv
            m_ref[h] = m_new

    n_lead = i % BLOCKS_PER_ITER
    for r in range(BLOCKS_PER_ITER):
        @pl.when(n_lead == r)
        def _(r=r):
            blocks_step([i] + list(range(r)), True)

    def body(u, carry):
        blocks_step([n_lead + BLOCKS_PER_ITER * u + t for t in range(BLOCKS_PER_ITER)], False)
        return carry

    lax.fori_loop(0, i // BLOCKS_PER_ITER, body, 0)

    outs = []
    for h in range(n_heads):
        acc = acc_ref[h]
        outs.append(acc[0:HEAD_DIM] * (1.0 / acc[HEAD_DIM:HEAD_DIM + 1]))
    ot = jnp.concatenate(outs, axis=0)
    o_ref[...] = (ot.T * ga_ref[...].astype(_F32)).astype(_BF16)


def _out_kernel(yp_ref, ya_ref, x_ref, w_ref, gain_ref, bias_ref, o_ref, *, d_pool, alpha, n_chunks):
    rows = x_ref.shape[0] // n_chunks
    for c in range(n_chunks):
        r = slice(c * rows, (c + 1) * rows)
        out = _dot(yp_ref[r, :], w_ref[0:d_pool, :]) + _dot(ya_ref[r, :], w_ref[d_pool:, :])
        z = alpha * x_ref[r, :] + out
        mu = jnp.mean(z, axis=-1, keepdims=True)
        zc = z - mu
        var = jnp.mean(zc * zc, axis=-1, keepdims=True)
        o_ref[r, :] = zc * lax.rsqrt(var + LN_EPS) * gain_ref[...] + bias_ref[...]


def _layer(h, positions, w_in, pool_w, pool_scale, w_out, ln_gain, ln_bias, alpha):
    b, s, d_model = h.shape
    d_pool = pool_scale.shape[0]
    d_attn = d_model - d_pool
    n_heads = d_attn // HEAD_DIM
    n_blocks = s // MOBA_BLOCK
    tm = ROW_TILE
    n_tiles = s // tm
    rows = b * s
    blocks_per_tile = tm // MOBA_BLOCK
    assert s % tm == 0 and tm % MOBA_BLOCK == 0 and d_attn % LANES == 0
    assert w_in.shape == (d_model, 2 * d_pool + 4 * d_attn)

    c_q = 2 * d_pool
    c_g = 2 * d_pool + 3 * d_attn
    w_std = jnp.concatenate([w_in[:, :c_q], w_in[:, c_g:]], axis=1).astype(_BF16)
    w_t = w_in[:, c_q:c_g].T.astype(_BF16)
    freqs = ROPE_THETA ** (-jnp.arange(ROT_HALF, dtype=_F32) * 2.0 / ROT_DIM)
    freq_tab = jnp.broadcast_to(freqs[:, None], (ROT_HALF, tm))
    x2 = h.reshape(rows, d_model)
    pos3 = positions.reshape(b, 1, s)

    proj = pl.pallas_call(
        functools.partial(_proj_kernel, d_pool=d_pool, d_attn=d_attn, n_heads=n_heads, tm=tm),
        grid=(b, n_tiles),
        in_specs=[
            pl.BlockSpec((tm, d_model), lambda bi, si: (bi * n_tiles + si, 0)),
            pl.BlockSpec((1, 1, tm), lambda bi, si: (bi, 0, si)),
            pl.BlockSpec((ROT_HALF, tm), lambda bi, si: (0, 0)),
            pl.BlockSpec(w_std.shape, lambda bi, si: (0, 0)),
            pl.BlockSpec(w_t.shape, lambda bi, si: (0, 0)),
            pl.BlockSpec(pool_w.shape, lambda bi, si: (0, 0, 0)),
            pl.BlockSpec((1, d_pool), lambda bi, si: (0, 0)),
        ],
        out_specs=[
            pl.BlockSpec((tm, d_pool), lambda bi, si: (bi * n_tiles + si, 0)),
            pl.BlockSpec((1, blocks_per_tile, d_attn, MOBA_BLOCK), lambda bi, si: (bi, si, 0, 0)),
            pl.BlockSpec((tm, d_attn), lambda bi, si: (bi * n_tiles + si, 0)),
            pl.BlockSpec((1, blocks_per_tile, d_attn, MOBA_BLOCK), lambda bi, si: (bi, si, 0, 0)),
            pl.BlockSpec((blocks_per_tile, 1, d_attn), lambda bi, si: (bi * n_tiles + si, 0, 0)),
            pl.BlockSpec((tm, d_attn), lambda bi, si: (bi * n_tiles + si, 0)),
        ],
        out_shape=[
            jax.ShapeDtypeStruct((rows, d_pool), _BF16),
            jax.ShapeDtypeStruct((b, n_blocks, d_attn, MOBA_BLOCK), _BF16),
            jax.ShapeDtypeStruct((rows, d_attn), _BF16),
            jax.ShapeDtypeStruct((b, n_blocks, d_attn, MOBA_BLOCK), _BF16),
            jax.ShapeDtypeStruct((b * n_blocks, 1, d_attn), _F32),
            jax.ShapeDtypeStruct((rows, d_attn), _BF16),
        ],
        scratch_shapes=[pltpu.VMEM((tm + POOL_HALO, d_pool), _F32)],
        compiler_params=pltpu.CompilerParams(
            dimension_semantics=("arbitrary", "arbitrary"), vmem_limit_bytes=VMEM_LIMIT),
        name="proj_pool_rope",
    )
    y_pool, q_t, k_rot, v_t, k_mean, g_attn = proj(
        x2, pos3, freq_tab, w_std, w_t, pool_w.astype(_BF16), pool_scale.reshape(1, d_pool))

    k_mean = k_mean.reshape(b, n_blocks, d_attn)
    attn = pl.pallas_call(
        functools.partial(_attn_kernel, n_blocks=n_blocks, n_heads=n_heads),
        grid=(b, n_blocks),
        in_specs=[
            pl.BlockSpec((1, 1, d_attn, MOBA_BLOCK), lambda bi, i: (bi, i, 0, 0)),
            pl.BlockSpec((s, d_attn), lambda bi, i: (bi, 0)),
            pl.BlockSpec((1, n_blocks, d_attn, MOBA_BLOCK), lambda bi, i: (bi, 0, 0, 0)),
            pl.BlockSpec((1, n_blocks, d_attn), lambda bi, i: (bi, 0, 0)),
            pl.BlockSpec((MOBA_BLOCK, d_attn), lambda bi, i: (bi * n_blocks + i, 0)),
        ],
        out_specs=pl.BlockSpec((MOBA_BLOCK, d_attn), lambda bi, i: (bi * n_blocks + i, 0)),
        out_shape=jax.ShapeDtypeStruct((rows, d_attn), _BF16),
        scratch_shapes=[
            pltpu.VMEM((n_heads, 2 * HEAD_DIM, MOBA_BLOCK), _BF16),
            pltpu.VMEM((n_heads, n_blocks, 1, MOBA_BLOCK), _F32),
            pltpu.VMEM((n_heads, 1, MOBA_BLOCK), _F32),
            pltpu.VMEM((n_heads, HEAD_DIM + ONES_ROWS, MOBA_BLOCK), _F32),
            pltpu.VMEM((BLOCKS_PER_ITER, n_heads, MOBA_BLOCK, MOBA_BLOCK), _F32),
        ],
        compiler_params=pltpu.CompilerParams(
            dimension_semantics=("arbitrary", "arbitrary"), vmem_limit_bytes=VMEM_LIMIT),
        name="moba_attention",
    )
    y_attn = attn(q_t, k_rot, v_t, k_mean, g_attn)

    outp = pl.pallas_call(
        functools.partial(_out_kernel, d_pool=d_pool, alpha=alpha, n_chunks=OUT_ROW_CHUNKS),
        grid=(rows // OUT_ROW_TILE,),
        in_specs=[
            pl.BlockSpec((OUT_ROW_TILE, d_pool), lambda r: (r, 0)),
            pl.BlockSpec((OUT_ROW_TILE, d_attn), lambda r: (r, 0)),
            pl.BlockSpec((OUT_ROW_TILE, d_model), lambda r: (r, 0)),
            pl.BlockSpec((d_model, d_model), lambda r: (0, 0)),
            pl.BlockSpec((1, d_model), lambda r: (0, 0)),
            pl.BlockSpec((1, d_model), lambda r: (0, 0)),
        ],
        out_specs=pl.BlockSpec((OUT_ROW_TILE, d_model), lambda r: (r, 0)),
        out_shape=jax.ShapeDtypeStruct((rows, d_model), h.dtype),
        compiler_params=pltpu.CompilerParams(
            dimension_semantics=("arbitrary",), vmem_limit_bytes=VMEM_LIMIT),
        name="out_proj_layernorm",
    )
    out = outp(y_pool, y_attn, x2, w_out.astype(_BF16),
               ln_gain.reshape(1, d_model), ln_bias.reshape(1, d_model))
    return out.reshape(b, s, d_model)


def kernel(x, positions, w_in, pool_w, pool_scale, w_out, ln_gain, ln_bias):
    depth = w_in.shape[0]
    alpha = (2.0 * depth) ** 0.25
    h = x
    for layer in range(depth):
        h = _layer(h, positions, w_in[layer], pool_w[layer], pool_scale[layer],
                   w_out[layer], ln_gain[layer], ln_bias[layer], alpha)
    return h
```

```python
import functools

import jax
import jax.numpy as jnp
from jax import lax
from jax.experimental import pallas as pl
from jax.experimental.pallas import tpu as pltpu

POOL_WINDOWS = (2, 4, 8, 16)
HEAD_DIM = 64
ROT_DIM = HEAD_DIM // 4
ROT_HALF = ROT_DIM // 2
ROPE_THETA = 500000.0
MOBA_BLOCK = 256
MOBA_TOPK = 3
LN_EPS = 1e-5

LANES = 128
POOL_HALO = 16
ROW_TILE = 1024
MASK_NEG = -1e30
ONES_ROWS = 16
LOG2_E = 1.4426950408889634
BLOCKS_PER_ITER = 4
QK_LOOKAHEAD = {1: 8, 2: 3, 3: 3, 4: 2}
OUT_ROW_TILE = 1024
OUT_ROW_CHUNKS = 4
VMEM_LIMIT = 48 * 1024 * 1024

_F32 = jnp.float32
_BF16 = jnp.bfloat16


def _silu(x):
    return x * (1.0 / (1.0 + jnp.exp(-x)))


def _dot(a, b):
    return jnp.dot(a, b, preferred_element_type=_F32)


def _dot_nt(a, b):
    return lax.dot_general(a, b, (((1,), (1,)), ((), ())), preferred_element_type=_F32)


def _rotate_heads(t, cos, sin, n_heads):
    pieces = []
    for h in range(n_heads):
        base = h * HEAD_DIM
        x1 = t[base:base + ROT_HALF]
        x2 = t[base + ROT_HALF:base + ROT_DIM]
        pieces.append(x1 * cos - x2 * sin)
        pieces.append(x2 * cos + x1 * sin)
        pieces.append(t[base + ROT_DIM:base + HEAD_DIM])
    return jnp.concatenate(pieces, axis=0)


def _proj_kernel(x_ref, pos_ref, freq_ref, wstd_ref, wt_ref, poolw_ref, pscale_ref,
                 ypool_ref, qt_ref, k_ref, vt_ref, km_ref, ga_ref, ubuf_ref,
                 *, d_pool, d_attn, n_heads, tm):
    s_idx = pl.program_id(1)

    @pl.when((pl.program_id(0) == 0) & (s_idx == 0))
    def _():
        ubuf_ref[0:POOL_HALO, :] = jnp.zeros((POOL_HALO, d_pool), _F32)

    xb = x_ref[...].astype(_BF16)

    kt_raw = _dot_nt(wt_ref[d_attn:2 * d_attn, :], xb)
    std = _dot(xb, wstd_ref[:, 0:2 * d_pool])
    qt_raw = _dot_nt(wt_ref[0:d_attn, :], xb)
    ga_raw = _dot(xb, wstd_ref[:, 2 * d_pool:])
    vt_raw = _dot_nt(wt_ref[2 * d_attn:3 * d_attn, :], xb)

    ubuf_ref[POOL_HALO:POOL_HALO + tm, :] = std[:, 0:d_pool]
    group = d_pool // len(POOL_WINDOWS)
    t_glob = s_idx * tm + lax.broadcasted_iota(jnp.int32, (tm, group), 0)
    for g, w in enumerate(POOL_WINDOWS):
        cols = slice(g * group, (g + 1) * group)
        u_g = ubuf_ref[POOL_HALO:POOL_HALO + tm, cols]
        tsum = u_g
        for s in range(1, w):
            tsum = tsum + ubuf_ref[POOL_HALO - s:POOL_HALO - s + tm, cols]
        count = jnp.minimum(t_glob + 1, w).astype(_F32)
        d = tsum / count - u_g
        y = _dot(d.astype(_BF16), poolw_ref[g])
        gate = _silu(std[:, d_pool + g * group:d_pool + (g + 1) * group])
        ypool_ref[:, cols] = (y * pscale_ref[:, cols] * gate).astype(_BF16)
    last_tile = s_idx == pl.num_programs(1) - 1
    ubuf_ref[0:POOL_HALO, :] = jnp.where(last_tile, 0.0, ubuf_ref[tm:tm + POOL_HALO, :])

    ga_ref[...] = _silu(ga_raw).astype(_BF16)

    ang = freq_ref[...] * pos_ref[0].astype(_F32)
    cos = jnp.cos(ang)
    sin = jnp.sin(ang)

    scale = HEAD_DIM ** -0.5 * LOG2_E
    qt = _rotate_heads(qt_raw, cos, sin, n_heads) * scale
    qt = qt.astype(_BF16)
    for r in range(tm // MOBA_BLOCK):
        qt_ref[0, r] = qt[:, r * MOBA_BLOCK:(r + 1) * MOBA_BLOCK]

    kt = _rotate_heads(kt_raw, cos, sin, n_heads)
    for c in range(d_attn // LANES):
        kc = kt[c * LANES:(c + 1) * LANES].T
        k_ref[:, c * LANES:(c + 1) * LANES] = kc.astype(_BF16)
        for r in range(tm // MOBA_BLOCK):
            blk_sum = jnp.sum(kc[r * MOBA_BLOCK:(r + 1) * MOBA_BLOCK], axis=0, keepdims=True)
            km_ref[r, :, c * LANES:(c + 1) * LANES] = blk_sum * (1.0 / MOBA_BLOCK)

    vt = vt_raw.astype(_BF16)
    for r in range(tm // MOBA_BLOCK):
        vt_ref[0, r] = vt[:, r * MOBA_BLOCK:(r + 1) * MOBA_BLOCK]


def _attn_kernel(qt_ref, k_ref, vt_ref, km_ref, ga_ref, o_ref,
                 qm_ref, bias_ref, m_ref, acc_ref, s_ref, *, n_blocks, n_heads):
    i = pl.program_id(1)
    blk = MOBA_BLOCK
    pair = 2 * HEAD_DIM

    zeros_half = jnp.zeros((HEAD_DIM, blk), _BF16)
    for h in range(n_heads):
        base = (h // 2) * pair
        if h % 2 == 0:
            qm_ref[h] = jnp.concatenate([qt_ref[0, 0, base:base + HEAD_DIM, :], zeros_half], axis=0)
        else:
            qm_ref[h] = jnp.concatenate([zeros_half, qt_ref[0, 0, base + HEAD_DIM:base + pair, :]], axis=0)

    ones_rows = jnp.ones((ONES_ROWS, blk), _BF16)
    key_pos = lax.broadcasted_iota(jnp.int32, (blk, blk), 0)
    qry_pos = lax.broadcasted_iota(jnp.int32, (blk, blk), 1)
    causal = key_pos <= qry_pos

    def blocks_step(js, own):
        k_pairs = []
        for j in js:
            row0 = pl.multiple_of(j * blk, blk)
            k_pairs.append([k_ref[pl.ds(row0, blk), p * pair:(p + 1) * pair] for p in range(n_heads // 2)])
        col_max = {}

        def issue_scores(h):
            for t in range(len(js)):
                s = _dot(k_pairs[t][h // 2], qm_ref[h])
                if own:
                    s = jnp.where(causal, s, MASK_NEG)
                s_ref[t, h] = s
                col_max[t, h] = jnp.max(s, axis=0, keepdims=True)

        ahead = min(QK_LOOKAHEAD[len(js)], n_heads)
        for h in range(ahead):
            issue_scores(h)
        for h in range(n_heads):
            if h + ahead < n_heads:
                issue_scores(h + ahead)
            bs = [None if own else bias_ref[h, j] for j in js]
            m_old = None if own else m_ref[h]
            m_new = m_old
            for t, b in enumerate(bs):
                cand = col_max[t, h] if b is None else col_max[t, h] + b
                m_new = cand if m_new is None else jnp.maximum(m_new, cand)
            pv = None
            for t, j in enumerate(js):
                p = jnp.exp2(s_ref[t, h] - m_new).astype(_BF16)
                vt = jnp.concatenate([vt_ref[0, j, h * HEAD_DIM:(h + 1) * HEAD_DIM, :], ones_rows], axis=0)
                c = _dot(vt, p)
                if bs[t] is not None:
                    c = jnp.where(bs[t] == 0.0, c, 0.0)
                pv = c if pv is None else pv + c
            if own:
                acc_ref[h] = pv
            else:
                acc_ref[h] = jnp.exp2(m_old - m_new) * acc_ref[h] + pv
            m_ref[h] = m_new

    def gate_and_lead(n_lead_blocks):
        blk_id = lax.broadcasted_iota(jnp.int32, (n_blocks, blk), 0)
        valid = blk_id < i
        gates = []
        for h in range(n_heads):
            base = (h // 2) * pair
            km = km_ref[0, :, base:base + pair]
            km_hi = km.astype(_BF16)
            km_lo = (km - km_hi.astype(_F32)).astype(_BF16)
            qm = qm_ref[h]
            gates.append(_dot(km_hi, qm) + _dot(km_lo, qm))

        blocks_step([i], True)

        for h in range(n_heads):
            g = jnp.where(valid, gates[h], -jnp.inf)
            picked = jnp.zeros((n_blocks, blk), _F32)
            for _ in range(MOBA_TOPK):
                g_max = jnp.max(g, axis=0, keepdims=True)
                first = jnp.min(jnp.where(g == g_max, blk_id, n_blocks), axis=0, keepdims=True)
                hit = blk_id == first
                picked = jnp.where(hit, 1.0, picked)
                g = jnp.where(hit, -jnp.inf, g)
            bias = jnp.where(valid, jnp.where(picked > 0.0, 0.0, MASK_NEG), MASK_NEG)
            for j in range(n_blocks):
                bias_ref[h, j] = bias[j:j + 1, :]

        if n_lead_blocks:
            blocks_step(list(range(n_lead_blocks)), False)

    n_lead = i % BLOCKS_PER_ITER
    for r in range(BLOCKS_PER_ITER):
        @pl.when(n_lead == r)
        def _(r=r):
            gate_and_lead(r)

    def body(u, carry):
        blocks_step([n_lead + BLOCKS_PER_ITER * u + t for t in range(BLOCKS_PER_ITER)], False)
        return carry

    lax.fori_loop(0, i // BLOCKS_PER_ITER, body, 0)

    outs = []
    for h in range(n_heads):
        acc = acc_ref[h]
        outs.append(acc[0:HEAD_DIM] * (1.0 / acc[HEAD_DIM:HEAD_DIM + 1]))
    ot = jnp.concatenate(outs, axis=0)
    o_ref[...] = (ot.T * ga_ref[...].astype(_F32)).astype(_BF16)


def _out_kernel(yp_ref, ya_ref, x_ref, w_ref, gain_ref, bias_ref, o_ref, *, d_pool, alpha, n_chunks):
    rows = x_ref.shape[0] // n_chunks
    for c in range(n_chunks):
        r = slice(c * rows, (c + 1) * rows)
        out = _dot(yp_ref[r, :], w_ref[0:d_pool, :]) + _dot(ya_ref[r, :], w_ref[d_pool:, :])
        z = alpha * x_ref[r, :] + out
        mu = jnp.mean(z, axis=-1, keepdims=True)
        zc = z - mu
        var = jnp.mean(zc * zc, axis=-1, keepdims=True)
        o_ref[r, :] = zc * lax.rsqrt(var + LN_EPS) * gain_ref[...] + bias_ref[...]


def _layer(h, positions, w_in, pool_w, pool_scale, w_out, ln_gain, ln_bias, alpha):
    b, s, d_model = h.shape
    d_pool = pool_scale.shape[0]
    d_attn = d_model - d_pool
    n_heads = d_attn // HEAD_DIM
    n_blocks = s // MOBA_BLOCK
    tm = ROW_TILE
    n_tiles = s // tm
    rows = b * s
    blocks_per_tile = tm // MOBA_BLOCK
    assert s % tm == 0 and tm % MOBA_BLOCK == 0 and d_attn % LANES == 0
    assert w_in.shape == (d_model, 2 * d_pool + 4 * d_attn)

    c_q = 2 * d_pool
    c_g = 2 * d_pool + 3 * d_attn
    w_std = jnp.concatenate([w_in[:, :c_q], w_in[:, c_g:]], axis=1).astype(_BF16)
    w_t = w_in[:, c_q:c_g].T.astype(_BF16)
    freqs = ROPE_THETA ** (-jnp.arange(ROT_HALF, dtype=_F32) * 2.0 / ROT_DIM)
    freq_tab = jnp.broadcast_to(freqs[:, None], (ROT_HALF, tm))
    x2 = h.reshape(rows, d_model)
    pos3 = positions.reshape(b, 1, s)

    proj = pl.pallas_call(
        functools.partial(_proj_kernel, d_pool=d_pool, d_attn=d_attn, n_heads=n_heads, tm=tm),
        grid=(b, n_tiles),
        in_specs=[
            pl.BlockSpec((tm, d_model), lambda bi, si: (bi * n_tiles + si, 0)),
            pl.BlockSpec((1, 1, tm), lambda bi, si: (bi, 0, si)),
            pl.BlockSpec((ROT_HALF, tm), lambda bi, si: (0, 0)),
            pl.BlockSpec(w_std.shape, lambda bi, si: (0, 0)),
            pl.BlockSpec(w_t.shape, lambda bi, si: (0, 0)),
            pl.BlockSpec(pool_w.shape, lambda bi, si: (0, 0, 0)),
            pl.BlockSpec((1, d_pool), lambda bi, si: (0, 0)),
        ],
        out_specs=[
            pl.BlockSpec((tm, d_pool), lambda bi, si: (bi * n_tiles + si, 0)),
            pl.BlockSpec((1, blocks_per_tile, d_attn, MOBA_BLOCK), lambda bi, si: (bi, si, 0, 0)),
            pl.BlockSpec((tm, d_attn), lambda bi, si: (bi * n_tiles + si, 0)),
            pl.BlockSpec((1, blocks_per_tile, d_attn, MOBA_BLOCK), lambda bi, si: (bi, si, 0, 0)),
            pl.BlockSpec((blocks_per_tile, 1, d_attn), lambda bi, si: (bi * n_tiles + si, 0, 0)),
            pl.BlockSpec((tm, d_attn), lambda bi, si: (bi * n_tiles + si, 0)),
        ],
        out_shape=[
            jax.ShapeDtypeStruct((rows, d_pool), _BF16),
            jax.ShapeDtypeStruct((b, n_blocks, d_attn, MOBA_BLOCK), _BF16),
            jax.ShapeDtypeStruct((rows, d_attn), _BF16),
            jax.ShapeDtypeStruct((b, n_blocks, d_attn, MOBA_BLOCK), _BF16),
            jax.ShapeDtypeStruct((b * n_blocks, 1, d_attn), _F32),
            jax.ShapeDtypeStruct((rows, d_attn), _BF16),
        ],
        scratch_shapes=[pltpu.VMEM((tm + POOL_HALO, d_pool), _F32)],
        compiler_params=pltpu.CompilerParams(
            dimension_semantics=("arbitrary", "arbitrary"), vmem_limit_bytes=VMEM_LIMIT),
        name="proj_pool_rope",
    )
    y_pool, q_t, k_rot, v_t, k_mean, g_attn = proj(
        x2, pos3, freq_tab, w_std, w_t, pool_w.astype(_BF16), pool_scale.reshape(1, d_pool))

    k_mean = k_mean.reshape(b, n_blocks, d_attn)
    attn = pl.pallas_call(
        functools.partial(_attn_kernel, n_blocks=n_blocks, n_heads=n_heads),
        grid=(b, n_blocks),
        in_specs=[
            pl.BlockSpec((1, 1, d_attn, MOBA_BLOCK), lambda bi, i: (bi, i, 0, 0)),
            pl.BlockSpec((s, d_attn), lambda bi, i: (bi, 0)),
            pl.BlockSpec((1, n_blocks, d_attn, MOBA_BLOCK), lambda bi, i: (bi, 0, 0, 0)),
            pl.BlockSpec((1, n_blocks, d_attn), lambda bi, i: (bi, 0, 0)),
            pl.BlockSpec((MOBA_BLOCK, d_attn), lambda bi, i: (bi * n_blocks + i, 0)),
        ],
        out_specs=pl.BlockSpec((MOBA_BLOCK, d_attn), lambda bi, i: (bi * n_blocks + i, 0)),
        out_shape=jax.ShapeDtypeStruct((rows, d_attn), _BF16),
        scratch_shapes=[
            pltpu.VMEM((n_heads, 2 * HEAD_DIM, MOBA_BLOCK), _BF16),
            pltpu.VMEM((n_heads, n_blocks, 1, MOBA_BLOCK), _F32),
            pltpu.VMEM((n_heads, 1, MOBA_BLOCK), _F32),
            pltpu.VMEM((n_heads, HEAD_DIM + ONES_ROWS, MOBA_BLOCK), _F32),
            pltpu.VMEM((BLOCKS_PER_ITER, n_heads, MOBA_BLOCK, MOBA_BLOCK), _F32),
        ],
        compiler_params=pltpu.CompilerParams(
            dimension_semantics=("arbitrary", "arbitrary"), vmem_limit_bytes=VMEM_LIMIT),
        name="moba_attention",
    )
    y_attn = attn(q_t, k_rot, v_t, k_mean, g_attn)

    outp = pl.pallas_call(
        functools.partial(_out_kernel, d_pool=d_pool, alpha=alpha, n_chunks=OUT_ROW_CHUNKS),
        grid=(rows // OUT_ROW_TILE,),
        in_specs=[
            pl.BlockSpec((OUT_ROW_TILE, d_pool), lambda r: (r, 0)),
            pl.BlockSpec((OUT_ROW_TILE, d_attn), lambda r: (r, 0)),
            pl.BlockSpec((OUT_ROW_TILE, d_model), lambda r: (r, 0)),
            pl.BlockSpec((d_model, d_model), lambda r: (0, 0)),
            pl.BlockSpec((1, d_model), lambda r: (0, 0)),
            pl.BlockSpec((1, d_model), lambda r: (0, 0)),
        ],
        out_specs=pl.BlockSpec((OUT_ROW_TILE, d_model), lambda r: (r, 0)),
        out_shape=jax.ShapeDtypeStruct((rows, d_model), h.dtype),
        compiler_params=pltpu.CompilerParams(
            dimension_semantics=("arbitrary",), vmem_limit_bytes=VMEM_LIMIT),
        name="out_proj_layernorm",
    )
    out = outp(y_pool, y_attn, x2, w_out.astype(_BF16),
               ln_gain.reshape(1, d_model), ln_bias.reshape(1, d_model))
    return out.reshape(b, s, d_model)


def kernel(x, positions, w_in, pool_w, pool_scale, w_out, ln_gain, ln_bias):
    depth = w_in.shape[0]
    alpha = (2.0 * depth) ** 0.25
    h = x
    for layer in range(depth):
        h = _layer(h, positions, w_in[layer], pool_w[layer], pool_scale[layer],
                   w_out[layer], ln_gain[layer], ln_bias[layer], alpha)
    return h
```

```python
import functools

import jax
import jax.numpy as jnp
from jax import lax
from jax.experimental import pallas as pl
from jax.experimental.pallas import tpu as pltpu

POOL_WINDOWS = (2, 4, 8, 16)
HEAD_DIM = 64
ROT_DIM = HEAD_DIM // 4
ROT_HALF = ROT_DIM // 2
ROPE_THETA = 500000.0
MOBA_BLOCK = 256
MOBA_TOPK = 3
LN_EPS = 1e-5

LANES = 128
POOL_HALO = 16
ROW_TILE = 1024
MASK_NEG = -1e30
ONES_ROWS = 16
LOG2_E = 1.4426950408889634
BLOCKS_PER_ITER = 4
QK_LOOKAHEAD = {1: 8, 2: 3, 3: 3, 4: 2}
OUT_ROW_TILE = 1024
OUT_ROW_CHUNKS = 4
VMEM_LIMIT = 48 * 1024 * 1024

_F32 = jnp.float32
_BF16 = jnp.bfloat16


def _silu(x):
    half = 0.5 * x
    return half + half * jnp.tanh(half)


def _dot(a, b):
    return jnp.dot(a, b, preferred_element_type=_F32)


def _dot_nt(a, b):
    return lax.dot_general(a, b, (((1,), (1,)), ((), ())), preferred_element_type=_F32)


def _rotate_heads(t, cos, sin, n_heads):
    pieces = []
    for h in range(n_heads):
        base = h * HEAD_DIM
        x1 = t[base:base + ROT_HALF]
        x2 = t[base + ROT_HALF:base + ROT_DIM]
        pieces.append(x1 * cos - x2 * sin)
        pieces.append(x2 * cos + x1 * sin)
        pieces.append(t[base + ROT_DIM:base + HEAD_DIM])
    return jnp.concatenate(pieces, axis=0)


def _proj_kernel(x_ref, pos_ref, freq_ref, wstd_ref, wt_ref, poolw_ref, pscale_ref,
                 ypool_ref, qt_ref, k_ref, vt_ref, km_ref, ga_ref, ubuf_ref,
                 *, d_pool, d_attn, n_heads, tm):
    s_idx = pl.program_id(1)

    @pl.when((pl.program_id(0) == 0) & (s_idx == 0))
    def _():
        ubuf_ref[0:POOL_HALO, :] = jnp.zeros((POOL_HALO, d_pool), _F32)

    xb = x_ref[...].astype(_BF16)

    kt_raw = _dot_nt(wt_ref[d_attn:2 * d_attn, :], xb)
    std = _dot(xb, wstd_ref[:, 0:2 * d_pool])
    qt_raw = _dot_nt(wt_ref[0:d_attn, :], xb)
    ga_raw = _dot(xb, wstd_ref[:, 2 * d_pool:])
    vt_raw = _dot_nt(wt_ref[2 * d_attn:3 * d_attn, :], xb)

    ubuf_ref[POOL_HALO:POOL_HALO + tm, :] = std[:, 0:d_pool]
    group = d_pool // len(POOL_WINDOWS)
    t_head = s_idx * tm + lax.broadcasted_iota(jnp.int32, (POOL_HALO, group), 0)
    for g, w in enumerate(POOL_WINDOWS):
        cols = slice(g * group, (g + 1) * group)
        u_g = ubuf_ref[POOL_HALO:POOL_HALO + tm, cols]
        tsum = u_g
        for s in range(1, w):
            tsum = tsum + ubuf_ref[POOL_HALO - s:POOL_HALO - s + tm, cols]
        count_head = jnp.minimum(t_head + 1, w).astype(_F32)
        mean_head = tsum[0:POOL_HALO] / count_head
        mean = jnp.concatenate([mean_head, tsum[POOL_HALO:] * (1.0 / w)], axis=0)
        d = mean - u_g
        y = _dot(d.astype(_BF16), poolw_ref[g])
        gate = _silu(std[:, d_pool + g * group:d_pool + (g + 1) * group])
        ypool_ref[:, cols] = (y * pscale_ref[:, cols] * gate).astype(_BF16)
    last_tile = s_idx == pl.num_programs(1) - 1
    ubuf_ref[0:POOL_HALO, :] = jnp.where(last_tile, 0.0, ubuf_ref[tm:tm + POOL_HALO, :])

    ga_ref[...] = _silu(ga_raw).astype(_BF16)

    ang = freq_ref[...] * pos_ref[0].astype(_F32)
    cos = jnp.cos(ang)
    sin = jnp.sin(ang)

    scale = HEAD_DIM ** -0.5 * LOG2_E
    qt = _rotate_heads(qt_raw, cos, sin, n_heads) * scale
    qt = qt.astype(_BF16)
    for r in range(tm // MOBA_BLOCK):
        qt_ref[0, r] = qt[:, r * MOBA_BLOCK:(r + 1) * MOBA_BLOCK]

    kt = _rotate_heads(kt_raw, cos, sin, n_heads)
    for c in range(d_attn // LANES):
        kc = kt[c * LANES:(c + 1) * LANES].T
        k_ref[:, c * LANES:(c + 1) * LANES] = kc.astype(_BF16)
        for r in range(tm // MOBA_BLOCK):
            blk_sum = jnp.sum(kc[r * MOBA_BLOCK:(r + 1) * MOBA_BLOCK], axis=0, keepdims=True)
            km_ref[r, :, c * LANES:(c + 1) * LANES] = blk_sum * (1.0 / MOBA_BLOCK)

    vt = vt_raw.astype(_BF16)
    for r in range(tm // MOBA_BLOCK):
        vt_ref[0, r] = vt[:, r * MOBA_BLOCK:(r + 1) * MOBA_BLOCK]


def _attn_kernel(qt_ref, k_ref, vt_ref, km_ref, ga_ref, o_ref,
                 qm_ref, bias_ref, m_ref, acc_ref, s_ref, *, n_blocks, n_heads):
    i = pl.program_id(1)
    blk = MOBA_BLOCK
    pair = 2 * HEAD_DIM

    zeros_half = jnp.zeros((HEAD_DIM, blk), _BF16)
    for h in range(n_heads):
        base = (h // 2) * pair
        if h % 2 == 0:
            qm_ref[h] = jnp.concatenate([qt_ref[0, 0, base:base + HEAD_DIM, :], zeros_half], axis=0)
        else:
            qm_ref[h] = jnp.concatenate([zeros_half, qt_ref[0, 0, base + HEAD_DIM:base + pair, :]], axis=0)

    ones_rows = jnp.ones((ONES_ROWS, blk), _BF16)
    key_pos = lax.broadcasted_iota(jnp.int32, (blk, blk), 0)
    qry_pos = lax.broadcasted_iota(jnp.int32, (blk, blk), 1)
    causal = key_pos <= qry_pos

    def blocks_step(js, own):
        k_pairs = []
        for j in js:
            row0 = pl.multiple_of(j * blk, blk)
            k_pairs.append([k_ref[pl.ds(row0, blk), p * pair:(p + 1) * pair] for p in range(n_heads // 2)])
        col_max = {}

        def issue_scores(h):
            for t in range(len(js)):
                s = _dot(k_pairs[t][h // 2], qm_ref[h])
                if own:
                    s = jnp.where(causal, s, MASK_NEG)
                s_ref[t, h] = s
                col_max[t, h] = jnp.max(s, axis=0, keepdims=True)

        ahead = min(QK_LOOKAHEAD[len(js)], n_heads)
        for h in range(ahead):
            issue_scores(h)
        for h in range(n_heads):
            if h + ahead < n_heads:
                issue_scores(h + ahead)
            bs = [None if own else bias_ref[h, j] for j in js]
            m_old = None if own else m_ref[h]
            m_new = m_old
            for t, b in enumerate(bs):
                cand = col_max[t, h] if b is None else col_max[t, h] + b
                m_new = cand if m_new is None else jnp.maximum(m_new, cand)
            pv = None
            for t, j in enumerate(js):
                p = jnp.exp2(s_ref[t, h] - m_new).astype(_BF16)
                vt = jnp.concatenate([vt_ref[0, j, h * HEAD_DIM:(h + 1) * HEAD_DIM, :], ones_rows], axis=0)
                c = _dot(vt, p)
                if bs[t] is not None:
                    c = jnp.where(bs[t] == 0.0, c, 0.0)
                pv = c if pv is None else pv + c
            if own:
                acc_ref[h] = pv
            else:
                acc_ref[h] = jnp.exp2(m_old - m_new) * acc_ref[h] + pv
            m_ref[h] = m_new

    def gate_and_lead(n_lead_blocks):
        blk_id = lax.broadcasted_iota(jnp.int32, (n_blocks, blk), 0)
        valid = blk_id < i
        gates = []
        for h in range(n_heads):
            base = (h // 2) * pair
            km = km_ref[0, :, base:base + pair]
            km_hi = km.astype(_BF16)
            km_lo = (km - km_hi.astype(_F32)).astype(_BF16)
            qm = qm_ref[h]
            gates.append(_dot(km_hi, qm) + _dot(km_lo, qm))

        blocks_step([i], True)

        for h in range(n_heads):
            g = jnp.where(valid, gates[h], -jnp.inf)
            picked = jnp.zeros((n_blocks, blk), _F32)
            for _ in range(MOBA_TOPK):
                g_max = jnp.max(g, axis=0, keepdims=True)
                first = jnp.min(jnp.where(g == g_max, blk_id, n_blocks), axis=0, keepdims=True)
                hit = blk_id == first
                picked = jnp.where(hit, 1.0, picked)
                g = jnp.where(hit, -jnp.inf, g)
            bias = jnp.where(valid, jnp.where(picked > 0.0, 0.0, MASK_NEG), MASK_NEG)
            for j in range(n_blocks):
                bias_ref[h, j] = bias[j:j + 1, :]

        if n_lead_blocks:
            blocks_step(list(range(n_lead_blocks)), False)

    n_lead = i % BLOCKS_PER_ITER
    for r in range(BLOCKS_PER_ITER):
        @pl.when(n_lead == r)
        def _(r=r):
            gate_and_lead(r)

    def body(u, carry):
        blocks_step([n_lead + BLOCKS_PER_ITER * u + t for t in range(BLOCKS_PER_ITER)], False)
        return carry

    lax.fori_loop(0, i // BLOCKS_PER_ITER, body, 0)

    outs = []
    for h in range(n_heads):
        acc = acc_ref[h]
        outs.append(acc[0:HEAD_DIM] * (1.0 / acc[HEAD_DIM:HEAD_DIM + 1]))
    ot = jnp.concatenate(outs, axis=0)
    o_ref[...] = (ot.T * ga_ref[...].astype(_F32)).astype(_BF16)


def _out_kernel(yp_ref, ya_ref, x_ref, w_ref, gain_ref, bias_ref, o_ref, *, d_pool, alpha, n_chunks):
    rows = x_ref.shape[0] // n_chunks
    for c in range(n_chunks):
        r = slice(c * rows, (c + 1) * rows)
        out = _dot(yp_ref[r, :], w_ref[0:d_pool, :]) + _dot(ya_ref[r, :], w_ref[d_pool:, :])
        z = alpha * x_ref[r, :] + out
        mu = jnp.mean(z, axis=-1, keepdims=True)
        zc = z - mu
        var = jnp.mean(zc * zc, axis=-1, keepdims=True)
        o_ref[r, :] = zc * lax.rsqrt(var + LN_EPS) * gain_ref[...] + bias_ref[...]


def _layer(h, positions, w_in, pool_w, pool_scale, w_out, ln_gain, ln_bias, alpha):
    b, s, d_model = h.shape
    d_pool = pool_scale.shape[0]
    d_attn = d_model - d_pool
    n_heads = d_attn // HEAD_DIM
    n_blocks = s // MOBA_BLOCK
    tm = ROW_TILE
    n_tiles = s // tm
    rows = b * s
    blocks_per_tile = tm // MOBA_BLOCK
    assert s % tm == 0 and tm % MOBA_BLOCK == 0 and d_attn % LANES == 0
    assert w_in.shape == (d_model, 2 * d_pool + 4 * d_attn)

    c_q = 2 * d_pool
    c_g = 2 * d_pool + 3 * d_attn
    w_std = jnp.concatenate([w_in[:, :c_q], w_in[:, c_g:]], axis=1).astype(_BF16)
    w_t = w_in[:, c_q:c_g].T.astype(_BF16)
    freqs = ROPE_THETA ** (-jnp.arange(ROT_HALF, dtype=_F32) * 2.0 / ROT_DIM)
    freq_tab = jnp.broadcast_to(freqs[:, None], (ROT_HALF, tm))
    x2 = h.reshape(rows, d_model)
    pos3 = positions.reshape(b, 1, s)

    proj = pl.pallas_call(
        functools.partial(_proj_kernel, d_pool=d_pool, d_attn=d_attn, n_heads=n_heads, tm=tm),
        grid=(b, n_tiles),
        in_specs=[
            pl.BlockSpec((tm, d_model), lambda bi, si: (bi * n_tiles + si, 0)),
            pl.BlockSpec((1, 1, tm), lambda bi, si: (bi, 0, si)),
            pl.BlockSpec((ROT_HALF, tm), lambda bi, si: (0, 0)),
            pl.BlockSpec(w_std.shape, lambda bi, si: (0, 0)),
            pl.BlockSpec(w_t.shape, lambda bi, si: (0, 0)),
            pl.BlockSpec(pool_w.shape, lambda bi, si: (0, 0, 0)),
            pl.BlockSpec((1, d_pool), lambda bi, si: (0, 0)),
        ],
        out_specs=[
            pl.BlockSpec((tm, d_pool), lambda bi, si: (bi * n_tiles + si, 0)),
            pl.BlockSpec((1, blocks_per_tile, d_attn, MOBA_BLOCK), lambda bi, si: (bi, si, 0, 0)),
            pl.BlockSpec((tm, d_attn), lambda bi, si: (bi * n_tiles + si, 0)),
            pl.BlockSpec((1, blocks_per_tile, d_attn, MOBA_BLOCK), lambda bi, si: (bi, si, 0, 0)),
            pl.BlockSpec((blocks_per_tile, 1, d_attn), lambda bi, si: (bi * n_tiles + si, 0, 0)),
            pl.BlockSpec((tm, d_attn), lambda bi, si: (bi * n_tiles + si, 0)),
        ],
        out_shape=[
            jax.ShapeDtypeStruct((rows, d_pool), _BF16),
            jax.ShapeDtypeStruct((b, n_blocks, d_attn, MOBA_BLOCK), _BF16),
            jax.ShapeDtypeStruct((rows, d_attn), _BF16),
            jax.ShapeDtypeStruct((b, n_blocks, d_attn, MOBA_BLOCK), _BF16),
            jax.ShapeDtypeStruct((b * n_blocks, 1, d_attn), _F32),
            jax.ShapeDtypeStruct((rows, d_attn), _BF16),
        ],
        scratch_shapes=[pltpu.VMEM((tm + POOL_HALO, d_pool), _F32)],
        compiler_params=pltpu.CompilerParams(
            dimension_semantics=("arbitrary", "arbitrary"), vmem_limit_bytes=VMEM_LIMIT),
        name="proj_pool_rope",
    )
    y_pool, q_t, k_rot, v_t, k_mean, g_attn = proj(
        x2, pos3, freq_tab, w_std, w_t, pool_w.astype(_BF16), pool_scale.reshape(1, d_pool))

    k_mean = k_mean.reshape(b, n_blocks, d_attn)
    attn = pl.pallas_call(
        functools.partial(_attn_kernel, n_blocks=n_blocks, n_heads=n_heads),
        grid=(b, n_blocks),
        in_specs=[
            pl.BlockSpec((1, 1, d_attn, MOBA_BLOCK), lambda bi, i: (bi, i, 0, 0)),
            pl.BlockSpec((s, d_attn), lambda bi, i: (bi, 0)),
            pl.BlockSpec((1, n_blocks, d_attn, MOBA_BLOCK), lambda bi, i: (bi, 0, 0, 0)),
            pl.BlockSpec((1, n_blocks, d_attn), lambda bi, i: (bi, 0, 0)),
            pl.BlockSpec((MOBA_BLOCK, d_attn), lambda bi, i: (bi * n_blocks + i, 0)),
        ],
        out_specs=pl.BlockSpec((MOBA_BLOCK, d_attn), lambda bi, i: (bi * n_blocks + i, 0)),
        out_shape=jax.ShapeDtypeStruct((rows, d_attn), _BF16),
        scratch_shapes=[
            pltpu.VMEM((n_heads, 2 * HEAD_DIM, MOBA_BLOCK), _BF16),
            pltpu.VMEM((n_heads, n_blocks, 1, MOBA_BLOCK), _F32),
            pltpu.VMEM((n_heads, 1, MOBA_BLOCK), _F32),
            pltpu.VMEM((n_heads, HEAD_DIM + ONES_ROWS, MOBA_BLOCK), _F32),
            pltpu.VMEM((BLOCKS_PER_ITER, n_heads, MOBA_BLOCK, MOBA_BLOCK), _F32),
        ],
        compiler_params=pltpu.CompilerParams(
            dimension_semantics=("arbitrary", "arbitrary"), vmem_limit_bytes=VMEM_LIMIT),
        name="moba_attention",
    )
    y_attn = attn(q_t, k_rot, v_t, k_mean, g_attn)

    outp = pl.pallas_call(
        functools.partial(_out_kernel, d_pool=d_pool, alpha=alpha, n_chunks=OUT_ROW_CHUNKS),
        grid=(rows // OUT_ROW_TILE,),
        in_specs=[
            pl.BlockSpec((OUT_ROW_TILE, d_pool), lambda r: (r, 0)),
            pl.BlockSpec((OUT_ROW_TILE, d_attn), lambda r: (r, 0)),
            pl.BlockSpec((OUT_ROW_TILE, d_model), lambda r: (r, 0)),
            pl.BlockSpec((d_model, d_model), lambda r: (0, 0)),
            pl.BlockSpec((1, d_model), lambda r: (0, 0)),
            pl.BlockSpec((1, d_model), lambda r: (0, 0)),
        ],
        out_specs=pl.BlockSpec((OUT_ROW_TILE, d_model), lambda r: (r, 0)),
        out_shape=jax.ShapeDtypeStruct((rows, d_model), h.dtype),
        compiler_params=pltpu.CompilerParams(
            dimension_semantics=("arbitrary",), vmem_limit_bytes=VMEM_LIMIT),
        name="out_proj_layernorm",
    )
    out = outp(y_pool, y_attn, x2, w_out.astype(_BF16),
               ln_gain.reshape(1, d_model), ln_bias.reshape(1, d_model))
    return out.reshape(b, s, d_model)


def kernel(x, positions, w_in, pool_w, pool_scale, w_out, ln_gain, ln_bias):
    depth = w_in.shape[0]
    alpha = (2.0 * depth) ** 0.25
    h = x
    for layer in range(depth):
        h = _layer(h, positions, w_in[layer], pool_w[layer], pool_scale[layer],
                   w_out[layer], ln_gain[layer], ln_bias[layer], alpha)
    return h
```

```python
import functools

import jax
import jax.numpy as jnp
from jax import lax
from jax.experimental import pallas as pl
from jax.experimental.pallas import tpu as pltpu

POOL_WINDOWS = (2, 4, 8, 16)
HEAD_DIM = 64
ROT_DIM = HEAD_DIM // 4
ROT_HALF = ROT_DIM // 2
ROPE_THETA = 500000.0
MOBA_BLOCK = 256
MOBA_TOPK = 3
LN_EPS = 1e-5

LANES = 128
POOL_HALO = 16
ROW_TILE = 1024
MASK_NEG = -1e30
ONES_ROWS = 16
LOG2_E = 1.4426950408889634
BLOCKS_PER_ITER = 4
QK_LOOKAHEAD = {1: 8, 2: 3, 3: 3, 4: 2}
OUT_ROW_TILE = 1024
VMEM_LIMIT = 48 * 1024 * 1024

_F32 = jnp.float32
_BF16 = jnp.bfloat16


def _silu(x):
    half = 0.5 * x
    return half + half * jnp.tanh(half)


def _dot(a, b):
    return jnp.dot(a, b, preferred_element_type=_F32)


def _dot_nt(a, b):
    return lax.dot_general(a, b, (((1,), (1,)), ((), ())), preferred_element_type=_F32)


def _dot_tn(a, b):
    return lax.dot_general(a, b, (((0,), (0,)), ((), ())), preferred_element_type=_F32)


def _rotate_heads(t, cos, sin, n_heads):
    pieces = []
    for h in range(n_heads):
        base = h * HEAD_DIM
        x1 = t[base:base + ROT_HALF]
        x2 = t[base + ROT_HALF:base + ROT_DIM]
        pieces.append(x1 * cos - x2 * sin)
        pieces.append(x2 * cos + x1 * sin)
        pieces.append(t[base + ROT_DIM:base + HEAD_DIM])
    return jnp.concatenate(pieces, axis=0)


def _proj_kernel(x_ref, pos_ref, freq_ref, wstd_ref, wt_ref, poolw_ref, pscale_ref,
                 ypool_ref, qt_ref, k_ref, vt_ref, km_ref, gat_ref, ubuf_ref,
                 *, d_pool, d_attn, n_heads, tm):
    s_idx = pl.program_id(1)

    @pl.when((pl.program_id(0) == 0) & (s_idx == 0))
    def _():
        ubuf_ref[0:POOL_HALO, :] = jnp.zeros((POOL_HALO, d_pool), _F32)

    xb = x_ref[...].astype(_BF16)

    kt_raw = _dot_nt(wt_ref[d_attn:2 * d_attn, :], xb)
    std = _dot(xb, wstd_ref[...])
    qt_raw = _dot_nt(wt_ref[0:d_attn, :], xb)
    gat_raw = _dot_nt(wt_ref[3 * d_attn:4 * d_attn, :], xb)
    vt_raw = _dot_nt(wt_ref[2 * d_attn:3 * d_attn, :], xb)

    ubuf_ref[POOL_HALO:POOL_HALO + tm, :] = std[:, 0:d_pool]
    group = d_pool // len(POOL_WINDOWS)
    t_head = s_idx * tm + lax.broadcasted_iota(jnp.int32, (POOL_HALO, group), 0)
    for g, w in enumerate(POOL_WINDOWS):
        cols = slice(g * group, (g + 1) * group)
        u_g = ubuf_ref[POOL_HALO:POOL_HALO + tm, cols]
        tsum = u_g
        for s in range(1, w):
            tsum = tsum + ubuf_ref[POOL_HALO - s:POOL_HALO - s + tm, cols]
        count_head = jnp.minimum(t_head + 1, w).astype(_F32)
        mean_head = tsum[0:POOL_HALO] / count_head
        mean = jnp.concatenate([mean_head, tsum[POOL_HALO:] * (1.0 / w)], axis=0)
        d = mean - u_g
        y = _dot(d.astype(_BF16), poolw_ref[g])
        gate = _silu(std[:, d_pool + g * group:d_pool + (g + 1) * group])
        ypool_ref[:, cols] = (y * pscale_ref[:, cols] * gate).astype(_BF16)
    last_tile = s_idx == pl.num_programs(1) - 1
    ubuf_ref[0:POOL_HALO, :] = jnp.where(last_tile, 0.0, ubuf_ref[tm:tm + POOL_HALO, :])

    gat = _silu(gat_raw).astype(_BF16)
    for r in range(tm // MOBA_BLOCK):
        gat_ref[0, r] = gat[:, r * MOBA_BLOCK:(r + 1) * MOBA_BLOCK]

    ang = freq_ref[...] * pos_ref[0].astype(_F32)
    cos = jnp.cos(ang)
    sin = jnp.sin(ang)

    scale = HEAD_DIM ** -0.5 * LOG2_E
    qt = _rotate_heads(qt_raw, cos, sin, n_heads) * scale
    qt = qt.astype(_BF16)
    for r in range(tm // MOBA_BLOCK):
        qt_ref[0, r] = qt[:, r * MOBA_BLOCK:(r + 1) * MOBA_BLOCK]

    kt = _rotate_heads(kt_raw, cos, sin, n_heads)
    for c in range(d_attn // LANES):
        kc = kt[c * LANES:(c + 1) * LANES].T
        k_ref[:, c * LANES:(c + 1) * LANES] = kc.astype(_BF16)
        for r in range(tm // MOBA_BLOCK):
            blk_sum = jnp.sum(kc[r * MOBA_BLOCK:(r + 1) * MOBA_BLOCK], axis=0, keepdims=True)
            km_ref[r, :, c * LANES:(c + 1) * LANES] = blk_sum * (1.0 / MOBA_BLOCK)

    vt = vt_raw.astype(_BF16)
    for r in range(tm // MOBA_BLOCK):
        vt_ref[0, r] = vt[:, r * MOBA_BLOCK:(r + 1) * MOBA_BLOCK]


def _attn_kernel(qt_ref, k_ref, vt_ref, km_ref, ga_ref, o_ref,
                 qm_ref, bias_ref, m_ref, acc_ref, s_ref, *, n_blocks, n_heads):
    i = pl.program_id(1)
    blk = MOBA_BLOCK
    pair = 2 * HEAD_DIM

    zeros_half = jnp.zeros((HEAD_DIM, blk), _BF16)
    for h in range(n_heads):
        base = (h // 2) * pair
        if h % 2 == 0:
            qm_ref[h] = jnp.concatenate([qt_ref[0, 0, base:base + HEAD_DIM, :], zeros_half], axis=0)
        else:
            qm_ref[h] = jnp.concatenate([zeros_half, qt_ref[0, 0, base + HEAD_DIM:base + pair, :]], axis=0)

    ones_rows = jnp.ones((ONES_ROWS, blk), _BF16)
    key_pos = lax.broadcasted_iota(jnp.int32, (blk, blk), 0)
    qry_pos = lax.broadcasted_iota(jnp.int32, (blk, blk), 1)
    causal = key_pos <= qry_pos

    def blocks_step(js, own):
        k_pairs = []
        for j in js:
            row0 = pl.multiple_of(j * blk, blk)
            k_pairs.append([k_ref[pl.ds(row0, blk), p * pair:(p + 1) * pair] for p in range(n_heads // 2)])
        col_max = {}

        def issue_scores(h):
            for t in range(len(js)):
                s = _dot(k_pairs[t][h // 2], qm_ref[h])
                if own:
                    s = jnp.where(causal, s, MASK_NEG)
                s_ref[t, h] = s
                col_max[t, h] = jnp.max(s, axis=0, keepdims=True)

        ahead = min(QK_LOOKAHEAD[len(js)], n_heads)
        for h in range(ahead):
            issue_scores(h)
        for h in range(n_heads):
            if h + ahead < n_heads:
                issue_scores(h + ahead)
            bs = [None if own else bias_ref[h, j] for j in js]
            m_old = None if own else m_ref[h]
            m_new = m_old
            for t, b in enumerate(bs):
                cand = col_max[t, h] if b is None else col_max[t, h] + b
                m_new = cand if m_new is None else jnp.maximum(m_new, cand)
            pv = None
            for t, j in enumerate(js):
                p = jnp.exp2(s_ref[t, h] - m_new).astype(_BF16)
                vt = jnp.concatenate([vt_ref[0, j, h * HEAD_DIM:(h + 1) * HEAD_DIM, :], ones_rows], axis=0)
                c = _dot(vt, p)
                if bs[t] is not None:
                    c = jnp.where(bs[t] == 0.0, c, 0.0)
                pv = c if pv is None else pv + c
            if own:
                acc_ref[h] = pv
            else:
                acc_ref[h] = jnp.exp2(m_old - m_new) * acc_ref[h] + pv
            m_ref[h] = m_new

    def gate_and_lead(n_lead_blocks):
        blk_id = lax.broadcasted_iota(jnp.int32, (n_blocks, blk), 0)
        valid = blk_id < i
        gates = []
        for h in range(n_heads):
            base = (h // 2) * pair
            km = km_ref[0, :, base:base + pair]
            km_hi = km.astype(_BF16)
            km_lo = (km - km_hi.astype(_F32)).astype(_BF16)
            qm = qm_ref[h]
            gates.append(_dot(km_hi, qm) + _dot(km_lo, qm))

        blocks_step([i], True)

        for h in range(n_heads):
            g = jnp.where(valid, gates[h], -jnp.inf)
            picked = jnp.zeros((n_blocks, blk), _F32)
            for _ in range(MOBA_TOPK):
                g_max = jnp.max(g, axis=0, keepdims=True)
                first = jnp.min(jnp.where(g == g_max, blk_id, n_blocks), axis=0, keepdims=True)
                hit = blk_id == first
                picked = jnp.where(hit, 1.0, picked)
                g = jnp.where(hit, -jnp.inf, g)
            bias = jnp.where(valid, jnp.where(picked > 0.0, 0.0, MASK_NEG), MASK_NEG)
            for j in range(n_blocks):
                bias_ref[h, j] = bias[j:j + 1, :]

        if n_lead_blocks:
            blocks_step(list(range(n_lead_blocks)), False)

    n_lead = i % BLOCKS_PER_ITER
    for r in range(BLOCKS_PER_ITER):
        @pl.when(n_lead == r)
        def _(r=r):
            gate_and_lead(r)

    def body(u, carry):
        blocks_step([n_lead + BLOCKS_PER_ITER * u + t for t in range(BLOCKS_PER_ITER)], False)
        return carry

    lax.fori_loop(0, i // BLOCKS_PER_ITER, body, 0)

    outs = []
    for h in range(n_heads):
        acc = acc_ref[h]
        outs.append(acc[0:HEAD_DIM] * (1.0 / acc[HEAD_DIM:HEAD_DIM + 1]))
    ot = jnp.concatenate(outs, axis=0)
    o_ref[0, 0] = (ot * ga_ref[0, 0].astype(_F32)).astype(_BF16)


def _out_kernel(yp_ref, yat_ref, x_ref, w_ref, gain_ref, bias_ref, o_ref, *, d_pool, alpha):
    rows = MOBA_BLOCK
    for c in range(x_ref.shape[0] // rows):
        r = slice(c * rows, (c + 1) * rows)
        out = _dot(yp_ref[r, :], w_ref[0:d_pool, :]) + _dot_tn(yat_ref[0, c], w_ref[d_pool:, :])
        z = alpha * x_ref[r, :] + out
        mu = jnp.mean(z, axis=-1, keepdims=True)
        zc = z - mu
        var = jnp.mean(zc * zc, axis=-1, keepdims=True)
        o_ref[r, :] = zc * lax.rsqrt(var + LN_EPS) * gain_ref[...] + bias_ref[...]


def _layer(h, positions, w_in, pool_w, pool_scale, w_out, ln_gain, ln_bias, alpha):
    b, s, d_model = h.shape
    d_pool = pool_scale.shape[0]
    d_attn = d_model - d_pool
    n_heads = d_attn // HEAD_DIM
    n_blocks = s // MOBA_BLOCK
    tm = ROW_TILE
    n_tiles = s // tm
    rows = b * s
    blocks_per_tile = tm // MOBA_BLOCK
    assert s % tm == 0 and tm % MOBA_BLOCK == 0 and d_attn % LANES == 0
    assert w_in.shape == (d_model, 2 * d_pool + 4 * d_attn)

    c_q = 2 * d_pool
    w_std = w_in[:, :c_q].astype(_BF16)
    w_t = w_in[:, c_q:].T.astype(_BF16)
    freqs = ROPE_THETA ** (-jnp.arange(ROT_HALF, dtype=_F32) * 2.0 / ROT_DIM)
    freq_tab = jnp.broadcast_to(freqs[:, None], (ROT_HALF, tm))
    x2 = h.reshape(rows, d_model)
    pos3 = positions.reshape(b, 1, s)

    proj = pl.pallas_call(
        functools.partial(_proj_kernel, d_pool=d_pool, d_attn=d_attn, n_heads=n_heads, tm=tm),
        grid=(b, n_tiles),
        in_specs=[
            pl.BlockSpec((tm, d_model), lambda bi, si: (bi * n_tiles + si, 0)),
            pl.BlockSpec((1, 1, tm), lambda bi, si: (bi, 0, si)),
            pl.BlockSpec((ROT_HALF, tm), lambda bi, si: (0, 0)),
            pl.BlockSpec(w_std.shape, lambda bi, si: (0, 0)),
            pl.BlockSpec(w_t.shape, lambda bi, si: (0, 0)),
            pl.BlockSpec(pool_w.shape, lambda bi, si: (0, 0, 0)),
            pl.BlockSpec((1, d_pool), lambda bi, si: (0, 0)),
        ],
        out_specs=[
            pl.BlockSpec((tm, d_pool), lambda bi, si: (bi * n_tiles + si, 0)),
            pl.BlockSpec((1, blocks_per_tile, d_attn, MOBA_BLOCK), lambda bi, si: (bi, si, 0, 0)),
            pl.BlockSpec((tm, d_attn), lambda bi, si: (bi * n_tiles + si, 0)),
            pl.BlockSpec((1, blocks_per_tile, d_attn, MOBA_BLOCK), lambda bi, si: (bi, si, 0, 0)),
            pl.BlockSpec((blocks_per_tile, 1, d_attn), lambda bi, si: (bi * n_tiles + si, 0, 0)),
            pl.BlockSpec((1, blocks_per_tile, d_attn, MOBA_BLOCK), lambda bi, si: (bi, si, 0, 0)),
        ],
        out_shape=[
            jax.ShapeDtypeStruct((rows, d_pool), _BF16),
            jax.ShapeDtypeStruct((b, n_blocks, d_attn, MOBA_BLOCK), _BF16),
            jax.ShapeDtypeStruct((rows, d_attn), _BF16),
            jax.ShapeDtypeStruct((b, n_blocks, d_attn, MOBA_BLOCK), _BF16),
            jax.ShapeDtypeStruct((b * n_blocks, 1, d_attn), _F32),
            jax.ShapeDtypeStruct((b, n_blocks, d_attn, MOBA_BLOCK), _BF16),
        ],
        scratch_shapes=[pltpu.VMEM((tm + POOL_HALO, d_pool), _F32)],
        compiler_params=pltpu.CompilerParams(
            dimension_semantics=("arbitrary", "arbitrary"), vmem_limit_bytes=VMEM_LIMIT),
        name="proj_pool_rope",
    )
    y_pool, q_t, k_rot, v_t, k_mean, g_attn = proj(
        x2, pos3, freq_tab, w_std, w_t, pool_w.astype(_BF16), pool_scale.reshape(1, d_pool))

    k_mean = k_mean.reshape(b, n_blocks, d_attn)
    attn = pl.pallas_call(
        functools.partial(_attn_kernel, n_blocks=n_blocks, n_heads=n_heads),
        grid=(b, n_blocks),
        in_specs=[
            pl.BlockSpec((1, 1, d_attn, MOBA_BLOCK), lambda bi, i: (bi, i, 0, 0)),
            pl.BlockSpec((s, d_attn), lambda bi, i: (bi, 0)),
            pl.BlockSpec((1, n_blocks, d_attn, MOBA_BLOCK), lambda bi, i: (bi, 0, 0, 0)),
            pl.BlockSpec((1, n_blocks, d_attn), lambda bi, i: (bi, 0, 0)),
            pl.BlockSpec((1, 1, d_attn, MOBA_BLOCK), lambda bi, i: (bi, i, 0, 0)),
        ],
        out_specs=pl.BlockSpec((1, 1, d_attn, MOBA_BLOCK), lambda bi, i: (bi, i, 0, 0)),
        out_shape=jax.ShapeDtypeStruct((b, n_blocks, d_attn, MOBA_BLOCK), _BF16),
        scratch_shapes=[
            pltpu.VMEM((n_heads, 2 * HEAD_DIM, MOBA_BLOCK), _BF16),
            pltpu.VMEM((n_heads, n_blocks, 1, MOBA_BLOCK), _F32),
            pltpu.VMEM((n_heads, 1, MOBA_BLOCK), _F32),
            pltpu.VMEM((n_heads, HEAD_DIM + ONES_ROWS, MOBA_BLOCK), _F32),
            pltpu.VMEM((BLOCKS_PER_ITER, n_heads, MOBA_BLOCK, MOBA_BLOCK), _F32),
        ],
        compiler_params=pltpu.CompilerParams(
            dimension_semantics=("arbitrary", "arbitrary"), vmem_limit_bytes=VMEM_LIMIT),
        name="moba_attention",
    )
    y_attn = attn(q_t, k_rot, v_t, k_mean, g_attn)

    out_blocks = OUT_ROW_TILE // MOBA_BLOCK
    out_tiles_per_seq = s // OUT_ROW_TILE
    assert s % OUT_ROW_TILE == 0 and OUT_ROW_TILE % MOBA_BLOCK == 0
    outp = pl.pallas_call(
        functools.partial(_out_kernel, d_pool=d_pool, alpha=alpha),
        grid=(rows // OUT_ROW_TILE,),
        in_specs=[
            pl.BlockSpec((OUT_ROW_TILE, d_pool), lambda r: (r, 0)),
            pl.BlockSpec((1, out_blocks, d_attn, MOBA_BLOCK),
                         lambda r: (r // out_tiles_per_seq, r % out_tiles_per_seq, 0, 0)),
            pl.BlockSpec((OUT_ROW_TILE, d_model), lambda r: (r, 0)),
            pl.BlockSpec((d_model, d_model), lambda r: (0, 0)),
            pl.BlockSpec((1, d_model), lambda r: (0, 0)),
            pl.BlockSpec((1, d_model), lambda r: (0, 0)),
        ],
        out_specs=pl.BlockSpec((OUT_ROW_TILE, d_model), lambda r: (r, 0)),
        out_shape=jax.ShapeDtypeStruct((rows, d_model), h.dtype),
        compiler_params=pltpu.CompilerParams(
            dimension_semantics=("arbitrary",), vmem_limit_bytes=VMEM_LIMIT),
        name="out_proj_layernorm",
    )
    out = outp(y_pool, y_attn, x2, w_out.astype(_BF16),
               ln_gain.reshape(1, d_model), ln_bias.reshape(1, d_model))
    return out.reshape(b, s, d_model)


def kernel(x, positions, w_in, pool_w, pool_scale, w_out, ln_gain, ln_bias):
    depth = w_in.shape[0]
    alpha = (2.0 * depth) ** 0.25
    h = x
    for layer in range(depth):
        h = _layer(h, positions, w_in[layer], pool_w[layer], pool_scale[layer],
                   w_out[layer], ln_gain[layer], ln_bias[layer], alpha)
    return h
```

```python
import functools

import jax
import jax.numpy as jnp
from jax import lax
from jax.experimental import pallas as pl
from jax.experimental.pallas import tpu as pltpu

POOL_WINDOWS = (2, 4, 8, 16)
HEAD_DIM = 64
ROT_DIM = HEAD_DIM // 4
ROT_HALF = ROT_DIM // 2
ROPE_THETA = 500000.0
MOBA_BLOCK = 256
MOBA_TOPK = 3
LN_EPS = 1e-5

LANES = 128
POOL_HALO = 16
ROW_TILE = 1024
MASK_NEG = -1e30
ONES_ROWS = 16
LOG2_E = 1.4426950408889634
BLOCKS_PER_ITER = 4
QK_LOOKAHEAD = {1: 8, 2: 3, 3: 3, 4: 2}
OUT_ROW_TILE = 1024
VMEM_LIMIT = 48 * 1024 * 1024

_F32 = jnp.float32
_BF16 = jnp.bfloat16


def _silu(x):
    half = 0.5 * x
    return half + half * jnp.tanh(half)


def _dot(a, b):
    return jnp.dot(a, b, preferred_element_type=_F32)


def _dot_nt(a, b):
    return lax.dot_general(a, b, (((1,), (1,)), ((), ())), preferred_element_type=_F32)


def _dot_tn(a, b):
    return lax.dot_general(a, b, (((0,), (0,)), ((), ())), preferred_element_type=_F32)


def _rotate_heads(t, cos, sin, n_heads):
    pieces = []
    for h in range(n_heads):
        base = h * HEAD_DIM
        x1 = t[base:base + ROT_HALF]
        x2 = t[base + ROT_HALF:base + ROT_DIM]
        pieces.append(x1 * cos - x2 * sin)
        pieces.append(x2 * cos + x1 * sin)
        pieces.append(t[base + ROT_DIM:base + HEAD_DIM])
    return jnp.concatenate(pieces, axis=0)


def _proj_kernel(x_ref, pos_ref, freq_ref, wstd_ref, wt_ref, poolw_ref, pscale_ref,
                 ypool_ref, qt_ref, k_ref, vt_ref, km_ref, gat_ref, ubuf_ref,
                 *, d_pool, d_attn, n_heads, tm):
    s_idx = pl.program_id(1)

    @pl.when((pl.program_id(0) == 0) & (s_idx == 0))
    def _():
        ubuf_ref[0:POOL_HALO, :] = jnp.zeros((POOL_HALO, d_pool), _F32)

    xb = x_ref[...].astype(_BF16)

    kt_raw = _dot_nt(wt_ref[d_attn:2 * d_attn, :], xb)
    std = _dot(xb, wstd_ref[...])
    qt_raw = _dot_nt(wt_ref[0:d_attn, :], xb)
    gat_raw = _dot_nt(wt_ref[3 * d_attn:4 * d_attn, :], xb)
    vt_raw = _dot_nt(wt_ref[2 * d_attn:3 * d_attn, :], xb)

    ubuf_ref[POOL_HALO:POOL_HALO + tm, :] = std[:, 0:d_pool]
    group = d_pool // len(POOL_WINDOWS)
    t_head = s_idx * tm + lax.broadcasted_iota(jnp.int32, (POOL_HALO, group), 0)
    for g, w in enumerate(POOL_WINDOWS):
        cols = slice(g * group, (g + 1) * group)
        u_g = ubuf_ref[POOL_HALO:POOL_HALO + tm, cols]
        tsum = u_g
        for s in range(1, w):
            tsum = tsum + ubuf_ref[POOL_HALO - s:POOL_HALO - s + tm, cols]
        count_head = jnp.minimum(t_head + 1, w).astype(_F32)
        mean_head = tsum[0:POOL_HALO] / count_head
        mean = jnp.concatenate([mean_head, tsum[POOL_HALO:] * (1.0 / w)], axis=0)
        d = mean - u_g
        y = _dot(d.astype(_BF16), poolw_ref[g])
        gate = _silu(std[:, d_pool + g * group:d_pool + (g + 1) * group])
        ypool_ref[:, cols] = (y * pscale_ref[:, cols] * gate).astype(_BF16)
    last_tile = s_idx == pl.num_programs(1) - 1
    ubuf_ref[0:POOL_HALO, :] = jnp.where(last_tile, 0.0, ubuf_ref[tm:tm + POOL_HALO, :])

    gat = _silu(gat_raw).astype(_BF16)
    for r in range(tm // MOBA_BLOCK):
        gat_ref[0, r] = gat[:, r * MOBA_BLOCK:(r + 1) * MOBA_BLOCK]

    ang = freq_ref[...] * pos_ref[0].astype(_F32)
    cos = jnp.cos(ang)
    sin = jnp.sin(ang)

    scale = HEAD_DIM ** -0.5 * LOG2_E
    qt = _rotate_heads(qt_raw, cos, sin, n_heads) * scale
    qt = qt.astype(_BF16)
    for r in range(tm // MOBA_BLOCK):
        qt_ref[0, r] = qt[:, r * MOBA_BLOCK:(r + 1) * MOBA_BLOCK]

    kt = _rotate_heads(kt_raw, cos, sin, n_heads)
    for c in range(d_attn // LANES):
        kc = kt[c * LANES:(c + 1) * LANES].T
        k_ref[:, c * LANES:(c + 1) * LANES] = kc.astype(_BF16)
        for r in range(tm // MOBA_BLOCK):
            blk_sum = jnp.sum(kc[r * MOBA_BLOCK:(r + 1) * MOBA_BLOCK], axis=0, keepdims=True)
            km_ref[r, :, c * LANES:(c + 1) * LANES] = blk_sum * (1.0 / MOBA_BLOCK)

    vt = vt_raw.astype(_BF16)
    for r in range(tm // MOBA_BLOCK):
        vt_ref[0, r] = vt[:, r * MOBA_BLOCK:(r + 1) * MOBA_BLOCK]


def _attn_kernel(qt_ref, k_ref, vt_ref, km_ref, ga_ref, o_ref,
                 qm_ref, bias_ref, m_ref, acc_ref, s_ref, *, n_blocks, n_heads):
    i = pl.program_id(1)
    blk = MOBA_BLOCK
    pair = 2 * HEAD_DIM

    zeros_half = jnp.zeros((HEAD_DIM, blk), _BF16)
    for h in range(n_heads):
        base = (h // 2) * pair
        if h % 2 == 0:
            qm_ref[h] = jnp.concatenate([qt_ref[0, 0, base:base + HEAD_DIM, :], zeros_half], axis=0)
        else:
            qm_ref[h] = jnp.concatenate([zeros_half, qt_ref[0, 0, base + HEAD_DIM:base + pair, :]], axis=0)

    ones_rows = jnp.ones((ONES_ROWS, blk), _BF16)
    key_pos = lax.broadcasted_iota(jnp.int32, (blk, blk), 0)
    qry_pos = lax.broadcasted_iota(jnp.int32, (blk, blk), 1)
    causal = key_pos <= qry_pos

    def blocks_step(js, own):
        k_pairs = []
        for j in js:
            row0 = pl.multiple_of(j * blk, blk)
            k_pairs.append([k_ref[pl.ds(row0, blk), p * pair:(p + 1) * pair] for p in range(n_heads // 2)])
        col_max = {}

        def issue_scores(h):
            for t in range(len(js)):
                s = _dot(k_pairs[t][h // 2], qm_ref[h])
                if own:
                    s = jnp.where(causal, s, MASK_NEG)
                s_ref[t, h] = s
                col_max[t, h] = jnp.max(s, axis=0, keepdims=True)

        ahead = min(QK_LOOKAHEAD[len(js)], n_heads)
        for h in range(ahead):
            issue_scores(h)
        for h in range(n_heads):
            if h + ahead < n_heads:
                issue_scores(h + ahead)
            bs = [None if own else bias_ref[h, j] for j in js]
            m_old = None if own else m_ref[h]
            m_new = m_old
            for t, b in enumerate(bs):
                cand = col_max[t, h] if b is None else col_max[t, h] + b
                m_new = cand if m_new is None else jnp.maximum(m_new, cand)
            pv = None
            for t, j in enumerate(js):
                p = jnp.exp2(s_ref[t, h] - m_new).astype(_BF16)
                vt = jnp.concatenate([vt_ref[0, j, h * HEAD_DIM:(h + 1) * HEAD_DIM, :], ones_rows], axis=0)
                c = _dot(vt, p)
                if bs[t] is not None:
                    c = jnp.where(bs[t] == 0.0, c, 0.0)
                pv = c if pv is None else pv + c
            if own:
                acc_ref[h] = pv
            else:
                acc_ref[h] = jnp.exp2(m_old - m_new) * acc_ref[h] + pv
            m_ref[h] = m_new

    def gate_and_lead(n_lead_blocks):
        blk_id = lax.broadcasted_iota(jnp.int32, (n_blocks, blk), 0)
        valid = blk_id < i
        gates = []
        for h in range(n_heads):
            base = (h // 2) * pair
            km = km_ref[0, :, base:base + pair]
            km_hi = km.astype(_BF16)
            km_lo = (km - km_hi.astype(_F32)).astype(_BF16)
            qm = qm_ref[h]
            gates.append(_dot(km_hi, qm) + _dot(km_lo, qm))

        blocks_step([i], True)

        for h in range(n_heads):
            g = jnp.where(valid, gates[h], -jnp.inf)
            picked = jnp.zeros((n_blocks, blk), _F32)
            for _ in range(MOBA_TOPK):
                g_max = jnp.max(g, axis=0, keepdims=True)
                first = jnp.min(jnp.where(g == g_max, blk_id, n_blocks), axis=0, keepdims=True)
                hit = blk_id == first
                picked = jnp.where(hit, 1.0, picked)
                g = jnp.where(hit, -jnp.inf, g)
            bias = jnp.where(valid, jnp.where(picked > 0.0, 0.0, MASK_NEG), MASK_NEG)
            for j in range(n_blocks):
                bias_ref[h, j] = bias[j:j + 1, :]

        start = 0
        while start < n_lead_blocks:
            size = (n_lead_blocks - start - 1) % BLOCKS_PER_ITER + 1
            blocks_step(list(range(start, start + size)), False)
            start += size

    n_lead = jnp.where(i >= BLOCKS_PER_ITER, i % BLOCKS_PER_ITER + BLOCKS_PER_ITER, i)
    for r in range(2 * BLOCKS_PER_ITER):
        @pl.when(n_lead == r)
        def _(r=r):
            gate_and_lead(r)

    def body(u, carry):
        blocks_step([n_lead + BLOCKS_PER_ITER * u + t for t in range(BLOCKS_PER_ITER)], False)
        return carry

    lax.fori_loop(0, (i - n_lead) // BLOCKS_PER_ITER, body, 0)

    outs = []
    for h in range(n_heads):
        acc = acc_ref[h]
        outs.append(acc[0:HEAD_DIM] * (1.0 / acc[HEAD_DIM:HEAD_DIM + 1]))
    ot = jnp.concatenate(outs, axis=0)
    o_ref[0, 0] = (ot * ga_ref[0, 0].astype(_F32)).astype(_BF16)


def _out_kernel(yp_ref, yat_ref, x_ref, w_ref, gain_ref, bias_ref, o_ref, *, d_pool, alpha):
    rows = MOBA_BLOCK
    for c in range(x_ref.shape[0] // rows):
        r = slice(c * rows, (c + 1) * rows)
        out = _dot(yp_ref[r, :], w_ref[0:d_pool, :]) + _dot_tn(yat_ref[0, c], w_ref[d_pool:, :])
        z = alpha * x_ref[r, :] + out
        mu = jnp.mean(z, axis=-1, keepdims=True)
        zc = z - mu
        var = jnp.mean(zc * zc, axis=-1, keepdims=True)
        o_ref[r, :] = zc * lax.rsqrt(var + LN_EPS) * gain_ref[...] + bias_ref[...]


def _layer(h, positions, w_in, pool_w, pool_scale, w_out, ln_gain, ln_bias, alpha):
    b, s, d_model = h.shape
    d_pool = pool_scale.shape[0]
    d_attn = d_model - d_pool
    n_heads = d_attn // HEAD_DIM
    n_blocks = s // MOBA_BLOCK
    tm = ROW_TILE
    n_tiles = s // tm
    rows = b * s
    blocks_per_tile = tm // MOBA_BLOCK
    assert s % tm == 0 and tm % MOBA_BLOCK == 0 and d_attn % LANES == 0
    assert w_in.shape == (d_model, 2 * d_pool + 4 * d_attn)

    c_q = 2 * d_pool
    w_std = w_in[:, :c_q].astype(_BF16)
    w_t = w_in[:, c_q:].T.astype(_BF16)
    freqs = ROPE_THETA ** (-jnp.arange(ROT_HALF, dtype=_F32) * 2.0 / ROT_DIM)
    freq_tab = jnp.broadcast_to(freqs[:, None], (ROT_HALF, tm))
    x2 = h.reshape(rows, d_model)
    pos3 = positions.reshape(b, 1, s)

    proj = pl.pallas_call(
        functools.partial(_proj_kernel, d_pool=d_pool, d_attn=d_attn, n_heads=n_heads, tm=tm),
        grid=(b, n_tiles),
        in_specs=[
            pl.BlockSpec((tm, d_model), lambda bi, si: (bi * n_tiles + si, 0)),
            pl.BlockSpec((1, 1, tm), lambda bi, si: (bi, 0, si)),
            pl.BlockSpec((ROT_HALF, tm), lambda bi, si: (0, 0)),
            pl.BlockSpec(w_std.shape, lambda bi, si: (0, 0)),
            pl.BlockSpec(w_t.shape, lambda bi, si: (0, 0)),
            pl.BlockSpec(pool_w.shape, lambda bi, si: (0, 0, 0)),
            pl.BlockSpec((1, d_pool), lambda bi, si: (0, 0)),
        ],
        out_specs=[
            pl.BlockSpec((tm, d_pool), lambda bi, si: (bi * n_tiles + si, 0)),
            pl.BlockSpec((1, blocks_per_tile, d_attn, MOBA_BLOCK), lambda bi, si: (bi, si, 0, 0)),
            pl.BlockSpec((tm, d_attn), lambda bi, si: (bi * n_tiles + si, 0)),
            pl.BlockSpec((1, blocks_per_tile, d_attn, MOBA_BLOCK), lambda bi, si: (bi, si, 0, 0)),
            pl.BlockSpec((blocks_per_tile, 1, d_attn), lambda bi, si: (bi * n_tiles + si, 0, 0)),
            pl.BlockSpec((1, blocks_per_tile, d_attn, MOBA_BLOCK), lambda bi, si: (bi, si, 0, 0)),
        ],
        out_shape=[
            jax.ShapeDtypeStruct((rows, d_pool), _BF16),
            jax.ShapeDtypeStruct((b, n_blocks, d_attn, MOBA_BLOCK), _BF16),
            jax.ShapeDtypeStruct((rows, d_attn), _BF16),
            jax.ShapeDtypeStruct((b, n_blocks, d_attn, MOBA_BLOCK), _BF16),
            jax.ShapeDtypeStruct((b * n_blocks, 1, d_attn), _F32),
            jax.ShapeDtypeStruct((b, n_blocks, d_attn, MOBA_BLOCK), _BF16),
        ],
        scratch_shapes=[pltpu.VMEM((tm + POOL_HALO, d_pool), _F32)],
        compiler_params=pltpu.CompilerParams(
            dimension_semantics=("arbitrary", "arbitrary"), vmem_limit_bytes=VMEM_LIMIT),
        name="proj_pool_rope",
    )
    y_pool, q_t, k_rot, v_t, k_mean, g_attn = proj(
        x2, pos3, freq_tab, w_std, w_t, pool_w.astype(_BF16), pool_scale.reshape(1, d_pool))

    k_mean = k_mean.reshape(b, n_blocks, d_attn)
    attn = pl.pallas_call(
        functools.partial(_attn_kernel, n_blocks=n_blocks, n_heads=n_heads),
        grid=(b, n_blocks),
        in_specs=[
            pl.BlockSpec((1, 1, d_attn, MOBA_BLOCK), lambda bi, i: (bi, i, 0, 0)),
            pl.BlockSpec((s, d_attn), lambda bi, i: (bi, 0)),
            pl.BlockSpec((1, n_blocks, d_attn, MOBA_BLOCK), lambda bi, i: (bi, 0, 0, 0)),
            pl.BlockSpec((1, n_blocks, d_attn), lambda bi, i: (bi, 0, 0)),
            pl.BlockSpec((1, 1, d_attn, MOBA_BLOCK), lambda bi, i: (bi, i, 0, 0)),
        ],
        out_specs=pl.BlockSpec((1, 1, d_attn, MOBA_BLOCK), lambda bi, i: (bi, i, 0, 0)),
        out_shape=jax.ShapeDtypeStruct((b, n_blocks, d_attn, MOBA_BLOCK), _BF16),
        scratch_shapes=[
            pltpu.VMEM((n_heads, 2 * HEAD_DIM, MOBA_BLOCK), _BF16),
            pltpu.VMEM((n_heads, n_blocks, 1, MOBA_BLOCK), _F32),
            pltpu.VMEM((n_heads, 1, MOBA_BLOCK), _F32),
            pltpu.VMEM((n_heads, HEAD_DIM + ONES_ROWS, MOBA_BLOCK), _F32),
            pltpu.VMEM((BLOCKS_PER_ITER, n_heads, MOBA_BLOCK, MOBA_BLOCK), _F32),
        ],
        compiler_params=pltpu.CompilerParams(
            dimension_semantics=("arbitrary", "arbitrary"), vmem_limit_bytes=VMEM_LIMIT),
        name="moba_attention",
    )
    y_attn = attn(q_t, k_rot, v_t, k_mean, g_attn)

    out_blocks = OUT_ROW_TILE // MOBA_BLOCK
    out_tiles_per_seq = s // OUT_ROW_TILE
    assert s % OUT_ROW_TILE == 0 and OUT_ROW_TILE % MOBA_BLOCK == 0
    outp = pl.pallas_call(
        functools.partial(_out_kernel, d_pool=d_pool, alpha=alpha),
        grid=(rows // OUT_ROW_TILE,),
        in_specs=[
            pl.BlockSpec((OUT_ROW_TILE, d_pool), lambda r: (r, 0)),
            pl.BlockSpec((1, out_blocks, d_attn, MOBA_BLOCK),
                         lambda r: (r // out_tiles_per_seq, r % out_tiles_per_seq, 0, 0)),
            pl.BlockSpec((OUT_ROW_TILE, d_model), lambda r: (r, 0)),
            pl.BlockSpec((d_model, d_model), lambda r: (0, 0)),
            pl.BlockSpec((1, d_model), lambda r: (0, 0)),
            pl.BlockSpec((1, d_model), lambda r: (0, 0)),
        ],
        out_specs=pl.BlockSpec((OUT_ROW_TILE, d_model), lambda r: (r, 0)),
        out_shape=jax.ShapeDtypeStruct((rows, d_model), h.dtype),
        compiler_params=pltpu.CompilerParams(
            dimension_semantics=("arbitrary",), vmem_limit_bytes=VMEM_LIMIT),
        name="out_proj_layernorm",
    )
    out = outp(y_pool, y_attn, x2, w_out.astype(_BF16),
               ln_gain.reshape(1, d_model), ln_bias.reshape(1, d_model))
    return out.reshape(b, s, d_model)


def kernel(x, positions, w_in, pool_w, pool_scale, w_out, ln_gain, ln_bias):
    depth = w_in.shape[0]
    alpha = (2.0 * depth) ** 0.25
    h = x
    for layer in range(depth):
        h = _layer(h, positions, w_in[layer], pool_w[layer], pool_scale[layer],
                   w_out[layer], ln_gain[layer], ln_bias[layer], alpha)
    return h
```

```python
import functools

import jax
import jax.numpy as jnp
from jax import lax
from jax.experimental import pallas as pl
from jax.experimental.pallas import tpu as pltpu

POOL_WINDOWS = (2, 4, 8, 16)
HEAD_DIM = 64
ROT_DIM = HEAD_DIM // 4
ROT_HALF = ROT_DIM // 2
ROPE_THETA = 500000.0
MOBA_BLOCK = 256
MOBA_TOPK = 3
LN_EPS = 1e-5

LANES = 128
POOL_HALO = 16
ROW_TILE = 1024
MASK_NEG = -1e30
ONES_ROWS = 16
LOG2_E = 1.4426950408889634
BLOCKS_PER_ITER = 4
QK_LOOKAHEAD = {1: 8, 2: 3, 3: 3, 4: 2}
OUT_ROW_TILE = 1024
VMEM_LIMIT = 48 * 1024 * 1024

_F32 = jnp.float32
_BF16 = jnp.bfloat16


def _silu(x):
    half = 0.5 * x
    return half + half * jnp.tanh(half)


def _dot(a, b):
    return jnp.dot(a, b, preferred_element_type=_F32)


def _dot_nt(a, b):
    return lax.dot_general(a, b, (((1,), (1,)), ((), ())), preferred_element_type=_F32)


def _dot_tn(a, b):
    return lax.dot_general(a, b, (((0,), (0,)), ((), ())), preferred_element_type=_F32)


def _rotate_heads(t, cos, sin, n_heads):
    pieces = []
    for h in range(n_heads):
        base = h * HEAD_DIM
        x1 = t[base:base + ROT_HALF]
        x2 = t[base + ROT_HALF:base + ROT_DIM]
        pieces.append(x1 * cos - x2 * sin)
        pieces.append(x2 * cos + x1 * sin)
        pieces.append(t[base + ROT_DIM:base + HEAD_DIM])
    return jnp.concatenate(pieces, axis=0)


def _proj_kernel(x_ref, pos_ref, freq_ref, wstd_ref, wt_ref, poolw_ref, pscale_ref,
                 ypool_ref, qt_ref, k_ref, vt_ref, km_ref, gat_ref, ubuf_ref,
                 *, d_pool, d_attn, n_heads, tm):
    s_idx = pl.program_id(1)

    @pl.when((pl.program_id(0) == 0) & (s_idx == 0))
    def _():
        ubuf_ref[0:POOL_HALO, :] = jnp.zeros((POOL_HALO, d_pool), _F32)

    xb = x_ref[...].astype(_BF16)

    kt_raw = _dot_nt(wt_ref[d_attn:2 * d_attn, :], xb)
    std = _dot(xb, wstd_ref[...])
    qt_raw = _dot_nt(wt_ref[0:d_attn, :], xb)
    gat_raw = _dot_nt(wt_ref[3 * d_attn:4 * d_attn, :], xb)
    vt_raw = _dot_nt(wt_ref[2 * d_attn:3 * d_attn, :], xb)

    ubuf_ref[POOL_HALO:POOL_HALO + tm, :] = std[:, 0:d_pool]
    group = d_pool // len(POOL_WINDOWS)
    t_head = s_idx * tm + lax.broadcasted_iota(jnp.int32, (POOL_HALO, group), 0)
    for g, w in enumerate(POOL_WINDOWS):
        cols = slice(g * group, (g + 1) * group)
        u_g = ubuf_ref[POOL_HALO:POOL_HALO + tm, cols]
        tsum = u_g
        for s in range(1, w):
            tsum = tsum + ubuf_ref[POOL_HALO - s:POOL_HALO - s + tm, cols]
        count_head = jnp.minimum(t_head + 1, w).astype(_F32)
        mean_head = tsum[0:POOL_HALO] / count_head
        mean = jnp.concatenate([mean_head, tsum[POOL_HALO:] * (1.0 / w)], axis=0)
        d = mean - u_g
        y = _dot(d.astype(_BF16), poolw_ref[g])
        gate = _silu(std[:, d_pool + g * group:d_pool + (g + 1) * group])
        ypool_ref[:, cols] = (y * pscale_ref[:, cols] * gate).astype(_BF16)
    last_tile = s_idx == pl.num_programs(1) - 1
    ubuf_ref[0:POOL_HALO, :] = jnp.where(last_tile, 0.0, ubuf_ref[tm:tm + POOL_HALO, :])

    gat = _silu(gat_raw).astype(_BF16)
    for r in range(tm // MOBA_BLOCK):
        gat_ref[0, r] = gat[:, r * MOBA_BLOCK:(r + 1) * MOBA_BLOCK]

    ang = freq_ref[...] * pos_ref[0].astype(_F32)
    cos = jnp.cos(ang)
    sin = jnp.sin(ang)

    scale = HEAD_DIM ** -0.5 * LOG2_E
    qt = _rotate_heads(qt_raw, cos, sin, n_heads) * scale
    qt = qt.astype(_BF16)
    for r in range(tm // MOBA_BLOCK):
        qt_ref[0, r] = qt[:, r * MOBA_BLOCK:(r + 1) * MOBA_BLOCK]

    kt = _rotate_heads(kt_raw, cos, sin, n_heads)
    for c in range(d_attn // LANES):
        kc = kt[c * LANES:(c + 1) * LANES].T
        k_ref[:, c * LANES:(c + 1) * LANES] = kc.astype(_BF16)
        for r in range(tm // MOBA_BLOCK):
            blk_sum = jnp.sum(kc[r * MOBA_BLOCK:(r + 1) * MOBA_BLOCK], axis=0, keepdims=True)
            km_ref[r, :, c * LANES:(c + 1) * LANES] = blk_sum * (1.0 / MOBA_BLOCK)

    vt = vt_raw.astype(_BF16)
    for r in range(tm // MOBA_BLOCK):
        vt_ref[0, r] = vt[:, r * MOBA_BLOCK:(r + 1) * MOBA_BLOCK]


def _attn_kernel(qt_ref, k_ref, vt_ref, km_ref, ga_ref, o_ref,
                 qm_ref, bias_ref, m_ref, acc_ref, s_ref, *, n_blocks, n_heads):
    i = pl.program_id(1)
    blk = MOBA_BLOCK
    pair = 2 * HEAD_DIM

    def build_masked_q():
        zeros_half = jnp.zeros((HEAD_DIM, blk), _BF16)
        for h in range(n_heads):
            base = (h // 2) * pair
            if h % 2 == 0:
                qm_ref[h] = jnp.concatenate([qt_ref[0, 0, base:base + HEAD_DIM, :], zeros_half], axis=0)
            else:
                qm_ref[h] = jnp.concatenate([zeros_half, qt_ref[0, 0, base + HEAD_DIM:base + pair, :]], axis=0)

    ones_rows = jnp.ones((ONES_ROWS, blk), _BF16)
    key_pos = lax.broadcasted_iota(jnp.int32, (blk, blk), 0)
    qry_pos = lax.broadcasted_iota(jnp.int32, (blk, blk), 1)
    causal = key_pos <= qry_pos

    def blocks_step(js, own):
        k_pairs = []
        for j in js:
            row0 = pl.multiple_of(j * blk, blk)
            k_pairs.append([k_ref[pl.ds(row0, blk), p * pair:(p + 1) * pair] for p in range(n_heads // 2)])
        col_max = {}

        def issue_scores(h):
            for t in range(len(js)):
                s = _dot(k_pairs[t][h // 2], qm_ref[h])
                if own:
                    s = jnp.where(causal, s, MASK_NEG)
                s_ref[t, h] = s
                col_max[t, h] = jnp.max(s, axis=0, keepdims=True)

        ahead = min(QK_LOOKAHEAD[len(js)], n_heads)
        for h in range(ahead):
            issue_scores(h)
        for h in range(n_heads):
            if h + ahead < n_heads:
                issue_scores(h + ahead)
            bs = [None if own else bias_ref[h, j] for j in js]
            m_old = None if own else m_ref[h]
            m_new = m_old
            for t, b in enumerate(bs):
                cand = col_max[t, h] if b is None else col_max[t, h] + b
                m_new = cand if m_new is None else jnp.maximum(m_new, cand)
            pv = None
            for t, j in enumerate(js):
                p = jnp.exp2(s_ref[t, h] - m_new).astype(_BF16)
                vt = jnp.concatenate([vt_ref[0, j, h * HEAD_DIM:(h + 1) * HEAD_DIM, :], ones_rows], axis=0)
                c = _dot(vt, p)
                if bs[t] is not None:
                    c = jnp.where(bs[t] == 0.0, c, 0.0)
                pv = c if pv is None else pv + c
            if own:
                acc_ref[h] = pv
            else:
                acc_ref[h] = jnp.exp2(m_old - m_new) * acc_ref[h] + pv
            m_ref[h] = m_new

    def gate_and_lead(n_lead_blocks):
        build_masked_q()
        blk_id = lax.broadcasted_iota(jnp.int32, (n_blocks, blk), 0)
        valid = blk_id < i
        gates = []
        for h in range(n_heads):
            base = (h // 2) * pair
            km = km_ref[0, :, base:base + pair]
            km_hi = km.astype(_BF16)
            km_lo = (km - km_hi.astype(_F32)).astype(_BF16)
            qm = qm_ref[h]
            gates.append(_dot(km_hi, qm) + _dot(km_lo, qm))

        blocks_step([i], True)

        for h in range(n_heads):
            g = jnp.where(valid, gates[h], -jnp.inf)
            picked = jnp.zeros((n_blocks, blk), _F32)
            for _ in range(MOBA_TOPK):
                g_max = jnp.max(g, axis=0, keepdims=True)
                first = jnp.min(jnp.where(g == g_max, blk_id, n_blocks), axis=0, keepdims=True)
                hit = blk_id == first
                picked = jnp.where(hit, 1.0, picked)
                g = jnp.where(hit, -jnp.inf, g)
            bias = jnp.where(valid, jnp.where(picked > 0.0, 0.0, MASK_NEG), MASK_NEG)
            for j in range(n_blocks):
                bias_ref[h, j] = bias[j:j + 1, :]

        start = 0
        while start < n_lead_blocks:
            size = (n_lead_blocks - start - 1) % BLOCKS_PER_ITER + 1
            blocks_step(list(range(start, start + size)), False)
            start += size

    n_lead = jnp.where(i >= BLOCKS_PER_ITER, i % BLOCKS_PER_ITER + BLOCKS_PER_ITER, i)
    for r in range(2 * BLOCKS_PER_ITER):
        @pl.when(n_lead == r)
        def _(r=r):
            gate_and_lead(r)

    def body(u, carry):
        blocks_step([n_lead + BLOCKS_PER_ITER * u + t for t in range(BLOCKS_PER_ITER)], False)
        return carry

    lax.fori_loop(0, (i - n_lead) // BLOCKS_PER_ITER, body, 0)

    outs = []
    for h in range(n_heads):
        acc = acc_ref[h]
        outs.append(acc[0:HEAD_DIM] * (1.0 / acc[HEAD_DIM:HEAD_DIM + 1]))
    ot = jnp.concatenate(outs, axis=0)
    o_ref[0, 0] = (ot * ga_ref[0, 0].astype(_F32)).astype(_BF16)


def _out_kernel(yp_ref, yat_ref, x_ref, w_ref, gain_ref, bias_ref, o_ref, *, d_pool, alpha):
    rows = MOBA_BLOCK
    for c in range(x_ref.shape[0] // rows):
        r = slice(c * rows, (c + 1) * rows)
        out = _dot(yp_ref[r, :], w_ref[0:d_pool, :]) + _dot_tn(yat_ref[0, c], w_ref[d_pool:, :])
        z = alpha * x_ref[r, :] + out
        mu = jnp.mean(z, axis=-1, keepdims=True)
        zc = z - mu
        var = jnp.mean(zc * zc, axis=-1, keepdims=True)
        o_ref[r, :] = zc * lax.rsqrt(var + LN_EPS) * gain_ref[...] + bias_ref[...]


def _layer(h, positions, w_in, pool_w, pool_scale, w_out, ln_gain, ln_bias, alpha):
    b, s, d_model = h.shape
    d_pool = pool_scale.shape[0]
    d_attn = d_model - d_pool
    n_heads = d_attn // HEAD_DIM
    n_blocks = s // MOBA_BLOCK
    tm = ROW_TILE
    n_tiles = s // tm
    rows = b * s
    blocks_per_tile = tm // MOBA_BLOCK
    assert s % tm == 0 and tm % MOBA_BLOCK == 0 and d_attn % LANES == 0
    assert w_in.shape == (d_model, 2 * d_pool + 4 * d_attn)

    c_q = 2 * d_pool
    w_std = w_in[:, :c_q].astype(_BF16)
    w_t = w_in[:, c_q:].T.astype(_BF16)
    freqs = ROPE_THETA ** (-jnp.arange(ROT_HALF, dtype=_F32) * 2.0 / ROT_DIM)
    freq_tab = jnp.broadcast_to(freqs[:, None], (ROT_HALF, tm))
    x2 = h.reshape(rows, d_model)
    pos3 = positions.reshape(b, 1, s)

    proj = pl.pallas_call(
        functools.partial(_proj_kernel, d_pool=d_pool, d_attn=d_attn, n_heads=n_heads, tm=tm),
        grid=(b, n_tiles),
        in_specs=[
            pl.BlockSpec((tm, d_model), lambda bi, si: (bi * n_tiles + si, 0)),
            pl.BlockSpec((1, 1, tm), lambda bi, si: (bi, 0, si)),
            pl.BlockSpec((ROT_HALF, tm), lambda bi, si: (0, 0)),
            pl.BlockSpec(w_std.shape, lambda bi, si: (0, 0)),
            pl.BlockSpec(w_t.shape, lambda bi, si: (0, 0)),
            pl.BlockSpec(pool_w.shape, lambda bi, si: (0, 0, 0)),
            pl.BlockSpec((1, d_pool), lambda bi, si: (0, 0)),
        ],
        out_specs=[
            pl.BlockSpec((tm, d_pool), lambda bi, si: (bi * n_tiles + si, 0)),
            pl.BlockSpec((1, blocks_per_tile, d_attn, MOBA_BLOCK), lambda bi, si: (bi, si, 0, 0)),
            pl.BlockSpec((tm, d_attn), lambda bi, si: (bi * n_tiles + si, 0)),
            pl.BlockSpec((1, blocks_per_tile, d_attn, MOBA_BLOCK), lambda bi, si: (bi, si, 0, 0)),
            pl.BlockSpec((blocks_per_tile, 1, d_attn), lambda bi, si: (bi * n_tiles + si, 0, 0)),
            pl.BlockSpec((1, blocks_per_tile, d_attn, MOBA_BLOCK), lambda bi, si: (bi, si, 0, 0)),
        ],
        out_shape=[
            jax.ShapeDtypeStruct((rows, d_pool), _BF16),
            jax.ShapeDtypeStruct((b, n_blocks, d_attn, MOBA_BLOCK), _BF16),
            jax.ShapeDtypeStruct((rows, d_attn), _BF16),
            jax.ShapeDtypeStruct((b, n_blocks, d_attn, MOBA_BLOCK), _BF16),
            jax.ShapeDtypeStruct((b * n_blocks, 1, d_attn), _F32),
            jax.ShapeDtypeStruct((b, n_blocks, d_attn, MOBA_BLOCK), _BF16),
        ],
        scratch_shapes=[pltpu.VMEM((tm + POOL_HALO, d_pool), _F32)],
        compiler_params=pltpu.CompilerParams(
            dimension_semantics=("arbitrary", "arbitrary"), vmem_limit_bytes=VMEM_LIMIT),
        name="proj_pool_rope",
    )
    y_pool, q_t, k_rot, v_t, k_mean, g_attn = proj(
        x2, pos3, freq_tab, w_std, w_t, pool_w.astype(_BF16), pool_scale.reshape(1, d_pool))

    k_mean = k_mean.reshape(b, n_blocks, d_attn)
    attn = pl.pallas_call(
        functools.partial(_attn_kernel, n_blocks=n_blocks, n_heads=n_heads),
        grid=(b, n_blocks),
        in_specs=[
            pl.BlockSpec((1, 1, d_attn, MOBA_BLOCK), lambda bi, i: (bi, i, 0, 0)),
            pl.BlockSpec((s, d_attn), lambda bi, i: (bi, 0)),
            pl.BlockSpec((1, n_blocks, d_attn, MOBA_BLOCK), lambda bi, i: (bi, 0, 0, 0)),
            pl.BlockSpec((1, n_blocks, d_attn), lambda bi, i: (bi, 0, 0)),
            pl.BlockSpec((1, 1, d_attn, MOBA_BLOCK), lambda bi, i: (bi, i, 0, 0)),
        ],
        out_specs=pl.BlockSpec((1, 1, d_attn, MOBA_BLOCK), lambda bi, i: (bi, i, 0, 0)),
        out_shape=jax.ShapeDtypeStruct((b, n_blocks, d_attn, MOBA_BLOCK), _BF16),
        scratch_shapes=[
            pltpu.VMEM((n_heads, 2 * HEAD_DIM, MOBA_BLOCK), _BF16),
            pltpu.VMEM((n_heads, n_blocks, 1, MOBA_BLOCK), _F32),
            pltpu.VMEM((n_heads, 1, MOBA_BLOCK), _F32),
            pltpu.VMEM((n_heads, HEAD_DIM + ONES_ROWS, MOBA_BLOCK), _F32),
            pltpu.VMEM((BLOCKS_PER_ITER, n_heads, MOBA_BLOCK, MOBA_BLOCK), _F32),
        ],
        compiler_params=pltpu.CompilerParams(
            dimension_semantics=("arbitrary", "arbitrary"), vmem_limit_bytes=VMEM_LIMIT),
        name="moba_attention",
    )
    y_attn = attn(q_t, k_rot, v_t, k_mean, g_attn)

    out_blocks = OUT_ROW_TILE // MOBA_BLOCK
    out_tiles_per_seq = s // OUT_ROW_TILE
    assert s % OUT_ROW_TILE == 0 and OUT_ROW_TILE % MOBA_BLOCK == 0
    outp = pl.pallas_call(
        functools.partial(_out_kernel, d_pool=d_pool, alpha=alpha),
        grid=(rows // OUT_ROW_TILE,),
        in_specs=[
            pl.BlockSpec((OUT_ROW_TILE, d_pool), lambda r: (r, 0)),
            pl.BlockSpec((1, out_blocks, d_attn, MOBA_BLOCK),
                         lambda r: (r // out_tiles_per_seq, r % out_tiles_per_seq, 0, 0)),
            pl.BlockSpec((OUT_ROW_TILE, d_model), lambda r: (r, 0)),
            pl.BlockSpec((d_model, d_model), lambda r: (0, 0)),
            pl.BlockSpec((1, d_model), lambda r: (0, 0)),
            pl.BlockSpec((1, d_model), lambda r: (0, 0)),
        ],
        out_specs=pl.BlockSpec((OUT_ROW_TILE, d_model), lambda r: (r, 0)),
        out_shape=jax.ShapeDtypeStruct((rows, d_model), h.dtype),
        compiler_params=pltpu.CompilerParams(
            dimension_semantics=("arbitrary",), vmem_limit_bytes=VMEM_LIMIT),
        name="out_proj_layernorm",
    )
    out = outp(y_pool, y_attn, x2, w_out.astype(_BF16),
               ln_gain.reshape(1, d_model), ln_bias.reshape(1, d_model))
    return out.reshape(b, s, d_model)


def kernel(x, positions, w_in, pool_w, pool_scale, w_out, ln_gain, ln_bias):
    depth = w_in.shape[0]
    alpha = (2.0 * depth) ** 0.25
    h = x
    for layer in range(depth):
        h = _layer(h, positions, w_in[layer], pool_w[layer], pool_scale[layer],
                   w_out[layer], ln_gain[layer], ln_bias[layer], alpha)
    return h
```

```python
import functools

import jax
import jax.numpy as jnp
from jax import lax
from jax.experimental import pallas as pl
from jax.experimental.pallas import tpu as pltpu

POOL_WINDOWS = (2, 4, 8, 16)
HEAD_DIM = 64
ROT_DIM = HEAD_DIM // 4
ROT_HALF = ROT_DIM // 2
ROPE_THETA = 500000.0
MOBA_BLOCK = 256
MOBA_TOPK = 3
LN_EPS = 1e-5

LANES = 128
POOL_HALO = 16
ROW_TILE = 1024
MASK_NEG = -1e30
ONES_ROWS = 16
LOG2_E = 1.4426950408889634
BLOCKS_PER_ITER = 4
QK_LOOKAHEAD = {1: 8, 2: 3, 3: 3, 4: 2}
OUT_ROW_TILE = 1024
VMEM_LIMIT = 48 * 1024 * 1024

_F32 = jnp.float32
_BF16 = jnp.bfloat16


def _silu(x):
    half = 0.5 * x
    return half + half * jnp.tanh(half)


def _dot(a, b):
    return jnp.dot(a, b, preferred_element_type=_F32)


def _dot_nt(a, b):
    return lax.dot_general(a, b, (((1,), (1,)), ((), ())), preferred_element_type=_F32)


def _dot_tn(a, b):
    return lax.dot_general(a, b, (((0,), (0,)), ((), ())), preferred_element_type=_F32)


def _rotate_heads(t, cos, sin, n_heads):
    pieces = []
    for h in range(n_heads):
        base = h * HEAD_DIM
        x1 = t[base:base + ROT_HALF]
        x2 = t[base + ROT_HALF:base + ROT_DIM]
        pieces.append(x1 * cos - x2 * sin)
        pieces.append(x2 * cos + x1 * sin)
        pieces.append(t[base + ROT_DIM:base + HEAD_DIM])
    return jnp.concatenate(pieces, axis=0)


def _proj_kernel(x_ref, pos_ref, freq_ref, wstd_ref, wt_ref, poolw_ref, pscale_ref,
                 ypool_ref, qt_ref, k_ref, vt_ref, km_ref, gat_ref, ubuf_ref,
                 *, d_pool, d_attn, n_heads, tm):
    s_idx = pl.program_id(1)

    @pl.when((pl.program_id(0) == 0) & (s_idx == 0))
    def _():
        ubuf_ref[0:POOL_HALO, :] = jnp.zeros((POOL_HALO, d_pool), _F32)

    xb = x_ref[...].astype(_BF16)

    kt_raw = _dot_nt(wt_ref[d_attn:2 * d_attn, :], xb)
    std = _dot(xb, wstd_ref[...])
    qt_raw = _dot_nt(wt_ref[0:d_attn, :], xb)
    gat_raw = _dot_nt(wt_ref[3 * d_attn:4 * d_attn, :], xb)
    vt_raw = _dot_nt(wt_ref[2 * d_attn:3 * d_attn, :], xb)

    ubuf_ref[POOL_HALO:POOL_HALO + tm, :] = std[:, 0:d_pool]
    group = d_pool // len(POOL_WINDOWS)
    t_head = s_idx * tm + lax.broadcasted_iota(jnp.int32, (POOL_HALO, group), 0)
    for g, w in enumerate(POOL_WINDOWS):
        cols = slice(g * group, (g + 1) * group)
        u_g = ubuf_ref[POOL_HALO:POOL_HALO + tm, cols]
        tsum = u_g
        for s in range(1, w):
            tsum = tsum + ubuf_ref[POOL_HALO - s:POOL_HALO - s + tm, cols]
        count_head = jnp.minimum(t_head + 1, w).astype(_F32)
        mean_head = tsum[0:POOL_HALO] / count_head
        mean = jnp.concatenate([mean_head, tsum[POOL_HALO:] * (1.0 / w)], axis=0)
        d = mean - u_g
        y = _dot(d.astype(_BF16), poolw_ref[g])
        gate = _silu(std[:, d_pool + g * group:d_pool + (g + 1) * group])
        ypool_ref[:, cols] = (y * pscale_ref[:, cols] * gate).astype(_BF16)
    last_tile = s_idx == pl.num_programs(1) - 1
    ubuf_ref[0:POOL_HALO, :] = jnp.where(last_tile, 0.0, ubuf_ref[tm:tm + POOL_HALO, :])

    gat = _silu(gat_raw).astype(_BF16)
    for r in range(tm // MOBA_BLOCK):
        gat_ref[0, r] = gat[:, r * MOBA_BLOCK:(r + 1) * MOBA_BLOCK]

    ang = freq_ref[...] * pos_ref[0].astype(_F32)
    cos = jnp.cos(ang)
    sin = jnp.sin(ang)

    scale = HEAD_DIM ** -0.5 * LOG2_E
    qt = _rotate_heads(qt_raw, cos, sin, n_heads) * scale
    qt = qt.astype(_BF16)
    for r in range(tm // MOBA_BLOCK):
        qt_ref[0, r] = qt[:, r * MOBA_BLOCK:(r + 1) * MOBA_BLOCK]

    kt = _rotate_heads(kt_raw, cos, sin, n_heads)
    for c in range(d_attn // LANES):
        kc = kt[c * LANES:(c + 1) * LANES].T
        k_ref[:, c * LANES:(c + 1) * LANES] = kc.astype(_BF16)
        for r in range(tm // MOBA_BLOCK):
            blk_sum = jnp.sum(kc[r * MOBA_BLOCK:(r + 1) * MOBA_BLOCK], axis=0, keepdims=True)
            km_ref[r, :, c * LANES:(c + 1) * LANES] = blk_sum * (1.0 / MOBA_BLOCK)

    vt = vt_raw.astype(_BF16)
    for r in range(tm // MOBA_BLOCK):
        vt_ref[0, r] = vt[:, r * MOBA_BLOCK:(r + 1) * MOBA_BLOCK]


def _attn_kernel(qt_ref, k_ref, vt_ref, km_ref, ga_ref, o_ref,
                 qm_ref, bias_ref, m_ref, acc_ref, s_ref, *, n_blocks, n_heads):
    i = pl.program_id(1)
    blk = MOBA_BLOCK
    pair = 2 * HEAD_DIM

    zeros_half = jnp.zeros((HEAD_DIM, blk), _BF16)
    for h in range(n_heads):
        base = (h // 2) * pair
        if h % 2 == 0:
            qm_ref[h] = jnp.concatenate([qt_ref[0, 0, base:base + HEAD_DIM, :], zeros_half], axis=0)
        else:
            qm_ref[h] = jnp.concatenate([zeros_half, qt_ref[0, 0, base + HEAD_DIM:base + pair, :]], axis=0)

    ones_rows = jnp.ones((ONES_ROWS, blk), _BF16)
    key_pos = lax.broadcasted_iota(jnp.int32, (blk, blk), 0)
    qry_pos = lax.broadcasted_iota(jnp.int32, (blk, blk), 1)
    causal = key_pos <= qry_pos

    def blocks_step(js, own):
        k_pairs = []
        for j in js:
            row0 = pl.multiple_of(j * blk, blk)
            k_pairs.append([k_ref[pl.ds(row0, blk), p * pair:(p + 1) * pair] for p in range(n_heads // 2)])
        col_max = {}

        def issue_scores(h):
            for t in range(len(js)):
                s = _dot(k_pairs[t][h // 2], qm_ref[h])
                if own:
                    s = jnp.where(causal, s, MASK_NEG)
                s_ref[t, h] = s
                col_max[t, h] = jnp.max(s, axis=0, keepdims=True)

        ahead = min(QK_LOOKAHEAD[len(js)], n_heads)
        for h in range(ahead):
            issue_scores(h)
        for h in range(n_heads):
            if h + ahead < n_heads:
                issue_scores(h + ahead)
            bs = [None if own else bias_ref[h, j] for j in js]
            m_old = None if own else m_ref[h]
            m_new = m_old
            for t, b in enumerate(bs):
                cand = col_max[t, h] if b is None else col_max[t, h] + b
                m_new = cand if m_new is None else jnp.maximum(m_new, cand)
            pv = None
            for t, j in enumerate(js):
                p = jnp.exp2(s_ref[t, h] - m_new).astype(_BF16)
                vt = jnp.concatenate([vt_ref[0, j, h * HEAD_DIM:(h + 1) * HEAD_DIM, :], ones_rows], axis=0)
                c = _dot(vt, p)
                if bs[t] is not None:
                    c = jnp.where(bs[t] == 0.0, c, 0.0)
                pv = c if pv is None else pv + c
            if own:
                acc_ref[h] = pv
            else:
                acc_ref[h] = jnp.exp2(m_old - m_new) * acc_ref[h] + pv
            m_ref[h] = m_new

    def gate_and_lead(n_lead_blocks):
        blk_id = lax.broadcasted_iota(jnp.int32, (n_blocks, blk), 0)
        valid = blk_id < i
        gates = []
        for h in range(n_heads):
            base = (h // 2) * pair
            km = km_ref[0, :, base:base + pair]
            km_hi = km.astype(_BF16)
            km_lo = (km - km_hi.astype(_F32)).astype(_BF16)
            qm = qm_ref[h]
            gates.append(_dot(km_hi, qm) + _dot(km_lo, qm))

        blocks_step([i], True)

        for h in range(n_heads):
            g = jnp.where(valid, gates[h], -jnp.inf)
            picked = jnp.zeros((n_blocks, blk), _F32)
            for _ in range(MOBA_TOPK):
                g_max = jnp.max(g, axis=0, keepdims=True)
                first = jnp.min(jnp.where(g == g_max, blk_id, n_blocks), axis=0, keepdims=True)
                hit = blk_id == first
                picked = jnp.where(hit, 1.0, picked)
                g = jnp.where(hit, -jnp.inf, g)
            bias = jnp.where(valid, jnp.where(picked > 0.0, 0.0, MASK_NEG), MASK_NEG)
            for j in range(n_blocks):
                bias_ref[h, j] = bias[j:j + 1, :]

        start = 0
        while start < n_lead_blocks:
            size = (n_lead_blocks - start - 1) % BLOCKS_PER_ITER + 1
            blocks_step(list(range(start, start + size)), False)
            start += size

        write_output()

    def write_output():
        outs = []
        for h in range(n_heads):
            acc = acc_ref[h]
            outs.append(acc[0:HEAD_DIM] * (1.0 / acc[HEAD_DIM:HEAD_DIM + 1]))
        ot = jnp.concatenate(outs, axis=0)
        o_ref[0, 0] = (ot * ga_ref[0, 0].astype(_F32)).astype(_BF16)

    n_lead = jnp.where(i >= BLOCKS_PER_ITER, i % BLOCKS_PER_ITER + BLOCKS_PER_ITER, i)
    for r in range(2 * BLOCKS_PER_ITER):
        @pl.when(n_lead == r)
        def _(r=r):
            gate_and_lead(r)

    def body(u, carry):
        blocks_step([n_lead + BLOCKS_PER_ITER * u + t for t in range(BLOCKS_PER_ITER)], False)
        return carry

    n_iter = (i - n_lead) // BLOCKS_PER_ITER
    lax.fori_loop(0, n_iter, body, 0)

    @pl.when(n_iter > 0)
    def _():
        write_output()


def _out_kernel(yp_ref, yat_ref, x_ref, w_ref, gain_ref, bias_ref, o_ref, *, d_pool, alpha):
    rows = MOBA_BLOCK
    for c in range(x_ref.shape[0] // rows):
        r = slice(c * rows, (c + 1) * rows)
        out = _dot(yp_ref[r, :], w_ref[0:d_pool, :]) + _dot_tn(yat_ref[0, c], w_ref[d_pool:, :])
        z = alpha * x_ref[r, :] + out
        mu = jnp.mean(z, axis=-1, keepdims=True)
        zc = z - mu
        var = jnp.mean(zc * zc, axis=-1, keepdims=True)
        o_ref[r, :] = zc * lax.rsqrt(var + LN_EPS) * gain_ref[...] + bias_ref[...]


def _layer(h, positions, w_in, pool_w, pool_scale, w_out, ln_gain, ln_bias, alpha):
    b, s, d_model = h.shape
    d_pool = pool_scale.shape[0]
    d_attn = d_model - d_pool
    n_heads = d_attn // HEAD_DIM
    n_blocks = s // MOBA_BLOCK
    tm = ROW_TILE
    n_tiles = s // tm
    rows = b * s
    blocks_per_tile = tm // MOBA_BLOCK
    assert s % tm == 0 and tm % MOBA_BLOCK == 0 and d_attn % LANES == 0
    assert w_in.shape == (d_model, 2 * d_pool + 4 * d_attn)

    c_q = 2 * d_pool
    w_std = w_in[:, :c_q].astype(_BF16)
    w_t = w_in[:, c_q:].T.astype(_BF16)
    freqs = ROPE_THETA ** (-jnp.arange(ROT_HALF, dtype=_F32) * 2.0 / ROT_DIM)
    freq_tab = jnp.broadcast_to(freqs[:, None], (ROT_HALF, tm))
    x2 = h.reshape(rows, d_model)
    pos3 = positions.reshape(b, 1, s)

    proj = pl.pallas_call(
        functools.partial(_proj_kernel, d_pool=d_pool, d_attn=d_attn, n_heads=n_heads, tm=tm),
        grid=(b, n_tiles),
        in_specs=[
            pl.BlockSpec((tm, d_model), lambda bi, si: (bi * n_tiles + si, 0)),
            pl.BlockSpec((1, 1, tm), lambda bi, si: (bi, 0, si)),
            pl.BlockSpec((ROT_HALF, tm), lambda bi, si: (0, 0)),
            pl.BlockSpec(w_std.shape, lambda bi, si: (0, 0)),
            pl.BlockSpec(w_t.shape, lambda bi, si: (0, 0)),
            pl.BlockSpec(pool_w.shape, lambda bi, si: (0, 0, 0)),
            pl.BlockSpec((1, d_pool), lambda bi, si: (0, 0)),
        ],
        out_specs=[
            pl.BlockSpec((tm, d_pool), lambda bi, si: (bi * n_tiles + si, 0)),
            pl.BlockSpec((1, blocks_per_tile, d_attn, MOBA_BLOCK), lambda bi, si: (bi, si, 0, 0)),
            pl.BlockSpec((tm, d_attn), lambda bi, si: (bi * n_tiles + si, 0)),
            pl.BlockSpec((1, blocks_per_tile, d_attn, MOBA_BLOCK), lambda bi, si: (bi, si, 0, 0)),
            pl.BlockSpec((blocks_per_tile, 1, d_attn), lambda bi, si: (bi * n_tiles + si, 0, 0)),
            pl.BlockSpec((1, blocks_per_tile, d_attn, MOBA_BLOCK), lambda bi, si: (bi, si, 0, 0)),
        ],
        out_shape=[
            jax.ShapeDtypeStruct((rows, d_pool), _BF16),
            jax.ShapeDtypeStruct((b, n_blocks, d_attn, MOBA_BLOCK), _BF16),
            jax.ShapeDtypeStruct((rows, d_attn), _BF16),
            jax.ShapeDtypeStruct((b, n_blocks, d_attn, MOBA_BLOCK), _BF16),
            jax.ShapeDtypeStruct((b * n_blocks, 1, d_attn), _F32),
            jax.ShapeDtypeStruct((b, n_blocks, d_attn, MOBA_BLOCK), _BF16),
        ],
        scratch_shapes=[pltpu.VMEM((tm + POOL_HALO, d_pool), _F32)],
        compiler_params=pltpu.CompilerParams(
            dimension_semantics=("arbitrary", "arbitrary"), vmem_limit_bytes=VMEM_LIMIT),
        name="proj_pool_rope",
    )
    y_pool, q_t, k_rot, v_t, k_mean, g_attn = proj(
        x2, pos3, freq_tab, w_std, w_t, pool_w.astype(_BF16), pool_scale.reshape(1, d_pool))

    k_mean = k_mean.reshape(b, n_blocks, d_attn)
    attn = pl.pallas_call(
        functools.partial(_attn_kernel, n_blocks=n_blocks, n_heads=n_heads),
        grid=(b, n_blocks),
        in_specs=[
            pl.BlockSpec((1, 1, d_attn, MOBA_BLOCK), lambda bi, i: (bi, i, 0, 0)),
            pl.BlockSpec((s, d_attn), lambda bi, i: (bi, 0)),
            pl.BlockSpec((1, n_blocks, d_attn, MOBA_BLOCK), lambda bi, i: (bi, 0, 0, 0)),
            pl.BlockSpec((1, n_blocks, d_attn), lambda bi, i: (bi, 0, 0)),
            pl.BlockSpec((1, 1, d_attn, MOBA_BLOCK), lambda bi, i: (bi, i, 0, 0)),
        ],
        out_specs=pl.BlockSpec((1, 1, d_attn, MOBA_BLOCK), lambda bi, i: (bi, i, 0, 0)),
        out_shape=jax.ShapeDtypeStruct((b, n_blocks, d_attn, MOBA_BLOCK), _BF16),
        scratch_shapes=[
            pltpu.VMEM((n_heads, 2 * HEAD_DIM, MOBA_BLOCK), _BF16),
            pltpu.VMEM((n_heads, n_blocks, 1, MOBA_BLOCK), _F32),
            pltpu.VMEM((n_heads, 1, MOBA_BLOCK), _F32),
            pltpu.VMEM((n_heads, HEAD_DIM + ONES_ROWS, MOBA_BLOCK), _F32),
            pltpu.VMEM((BLOCKS_PER_ITER, n_heads, MOBA_BLOCK, MOBA_BLOCK), _F32),
        ],
        compiler_params=pltpu.CompilerParams(
            dimension_semantics=("arbitrary", "arbitrary"), vmem_limit_bytes=VMEM_LIMIT),
        name="moba_attention",
    )
    y_attn = attn(q_t, k_rot, v_t, k_mean, g_attn)

    out_blocks = OUT_ROW_TILE // MOBA_BLOCK
    out_tiles_per_seq = s // OUT_ROW_TILE
    assert s % OUT_ROW_TILE == 0 and OUT_ROW_TILE % MOBA_BLOCK == 0
    outp = pl.pallas_call(
        functools.partial(_out_kernel, d_pool=d_pool, alpha=alpha),
        grid=(rows // OUT_ROW_TILE,),
        in_specs=[
            pl.BlockSpec((OUT_ROW_TILE, d_pool), lambda r: (r, 0)),
            pl.BlockSpec((1, out_blocks, d_attn, MOBA_BLOCK),
                         lambda r: (r // out_tiles_per_seq, r % out_tiles_per_seq, 0, 0)),
            pl.BlockSpec((OUT_ROW_TILE, d_model), lambda r: (r, 0)),
            pl.BlockSpec((d_model, d_model), lambda r: (0, 0)),
            pl.BlockSpec((1, d_model), lambda r: (0, 0)),
            pl.BlockSpec((1, d_model), lambda r: (0, 0)),
        ],
        out_specs=pl.BlockSpec((OUT_ROW_TILE, d_model), lambda r: (r, 0)),
        out_shape=jax.ShapeDtypeStruct((rows, d_model), h.dtype),
        compiler_params=pltpu.CompilerParams(
            dimension_semantics=("arbitrary",), vmem_limit_bytes=VMEM_LIMIT),
        name="out_proj_layernorm",
    )
    out = outp(y_pool, y_attn, x2, w_out.astype(_BF16),
               ln_gain.reshape(1, d_model), ln_bias.reshape(1, d_model))
    return out.reshape(b, s, d_model)


def kernel(x, positions, w_in, pool_w, pool_scale, w_out, ln_gain, ln_bias):
    depth = w_in.shape[0]
    alpha = (2.0 * depth) ** 0.25
    h = x
    for layer in range(depth):
        h = _layer(h, positions, w_in[layer], pool_w[layer], pool_scale[layer],
                   w_out[layer], ln_gain[layer], ln_bias[layer], alpha)
    return h
```

```python
import functools

import jax
import jax.numpy as jnp
from jax import lax
from jax.experimental import pallas as pl
from jax.experimental.pallas import tpu as pltpu

POOL_WINDOWS = (2, 4, 8, 16)
HEAD_DIM = 64
ROT_DIM = HEAD_DIM // 4
ROT_HALF = ROT_DIM // 2
ROPE_THETA = 500000.0
MOBA_BLOCK = 256
MOBA_TOPK = 3
LN_EPS = 1e-5

LANES = 128
POOL_HALO = 16
ROW_TILE = 1024
MASK_NEG = -1e30
ONES_ROWS = 16
LOG2_E = 1.4426950408889634
BLOCKS_PER_ITER = 4
QK_LOOKAHEAD = {1: 8, 2: 3, 3: 3, 4: 2}
OUT_ROW_TILE = 1024
VMEM_LIMIT = 48 * 1024 * 1024

_F32 = jnp.float32
_BF16 = jnp.bfloat16


def _silu(x):
    half = 0.5 * x
    return half + half * jnp.tanh(half)


def _dot(a, b):
    return jnp.dot(a, b, preferred_element_type=_F32)


def _dot_nt(a, b):
    return lax.dot_general(a, b, (((1,), (1,)), ((), ())), preferred_element_type=_F32)


def _dot_tn(a, b):
    return lax.dot_general(a, b, (((0,), (0,)), ((), ())), preferred_element_type=_F32)


def _rotate_heads(t, cos, sin, n_heads):
    pieces = []
    for h in range(n_heads):
        base = h * HEAD_DIM
        x1 = t[base:base + ROT_HALF]
        x2 = t[base + ROT_HALF:base + ROT_DIM]
        pieces.append(x1 * cos - x2 * sin)
        pieces.append(x2 * cos + x1 * sin)
        pieces.append(t[base + ROT_DIM:base + HEAD_DIM])
    return jnp.concatenate(pieces, axis=0)


def _proj_kernel(x_ref, pos_ref, freq_ref, wstd_ref, wt_ref, poolw_ref, pscale_ref,
                 ypool_ref, qt_ref, k_ref, vt_ref, km_ref, gat_ref, ubuf_ref,
                 *, d_pool, d_attn, n_heads, tm):
    s_idx = pl.program_id(1)

    @pl.when((pl.program_id(0) == 0) & (s_idx == 0))
    def _():
        ubuf_ref[0:POOL_HALO, :] = jnp.zeros((POOL_HALO, d_pool), _F32)

    xb = x_ref[...].astype(_BF16)

    kt_raw = _dot_nt(wt_ref[d_attn:2 * d_attn, :], xb)
    std = _dot(xb, wstd_ref[...])
    qt_raw = _dot_nt(wt_ref[0:d_attn, :], xb)
    gat_raw = _dot_nt(wt_ref[3 * d_attn:4 * d_attn, :], xb)
    vt_raw = _dot_nt(wt_ref[2 * d_attn:3 * d_attn, :], xb)

    ubuf_ref[POOL_HALO:POOL_HALO + tm, :] = std[:, 0:d_pool]
    group = d_pool // len(POOL_WINDOWS)
    t_head = s_idx * tm + lax.broadcasted_iota(jnp.int32, (POOL_HALO, group), 0)
    for g, w in enumerate(POOL_WINDOWS):
        cols = slice(g * group, (g + 1) * group)
        u_g = ubuf_ref[POOL_HALO:POOL_HALO + tm, cols]
        tsum = u_g
        for s in range(1, w):
            tsum = tsum + ubuf_ref[POOL_HALO - s:POOL_HALO - s + tm, cols]
        count_head = jnp.minimum(t_head + 1, w).astype(_F32)
        mean_head = tsum[0:POOL_HALO] / count_head
        mean = jnp.concatenate([mean_head, tsum[POOL_HALO:] * (1.0 / w)], axis=0)
        d = mean - u_g
        y = _dot(d.astype(_BF16), poolw_ref[g])
        gate = _silu(std[:, d_pool + g * group:d_pool + (g + 1) * group])
        ypool_ref[:, cols] = (y * pscale_ref[:, cols] * gate).astype(_BF16)
    last_tile = s_idx == pl.num_programs(1) - 1
    ubuf_ref[0:POOL_HALO, :] = jnp.where(last_tile, 0.0, ubuf_ref[tm:tm + POOL_HALO, :])

    gat = _silu(gat_raw).astype(_BF16)
    for r in range(tm // MOBA_BLOCK):
        gat_ref[0, r] = gat[:, r * MOBA_BLOCK:(r + 1) * MOBA_BLOCK]

    ang = freq_ref[...] * pos_ref[0].astype(_F32)
    cos = jnp.cos(ang)
    sin = jnp.sin(ang)

    scale = HEAD_DIM ** -0.5 * LOG2_E
    qt = _rotate_heads(qt_raw, cos, sin, n_heads) * scale
    qt = qt.astype(_BF16)
    for r in range(tm // MOBA_BLOCK):
        qt_ref[0, r] = qt[:, r * MOBA_BLOCK:(r + 1) * MOBA_BLOCK]

    kt = _rotate_heads(kt_raw, cos, sin, n_heads)
    for c in range(d_attn // LANES):
        kc = kt[c * LANES:(c + 1) * LANES].T
        k_ref[:, c * LANES:(c + 1) * LANES] = kc.astype(_BF16)
        for r in range(tm // MOBA_BLOCK):
            blk_sum = jnp.sum(kc[r * MOBA_BLOCK:(r + 1) * MOBA_BLOCK], axis=0, keepdims=True)
            km_ref[r, :, c * LANES:(c + 1) * LANES] = blk_sum * (1.0 / MOBA_BLOCK)

    vt = vt_raw.astype(_BF16)
    for r in range(tm // MOBA_BLOCK):
        vt_ref[0, r] = vt[:, r * MOBA_BLOCK:(r + 1) * MOBA_BLOCK]


def _attn_kernel(qt_ref, k_ref, vt_ref, km_ref, ga_ref, o_ref,
                 qm_ref, bias_ref, m_ref, acc_ref, s_ref, *, n_blocks, n_heads):
    i = pl.program_id(1)
    blk = MOBA_BLOCK
    pair = 2 * HEAD_DIM

    zeros_half = jnp.zeros((HEAD_DIM, blk), _BF16)
    for h in range(n_heads):
        base = (h // 2) * pair
        if h % 2 == 0:
            qm_ref[h] = jnp.concatenate([qt_ref[0, 0, base:base + HEAD_DIM, :], zeros_half], axis=0)
        else:
            qm_ref[h] = jnp.concatenate([zeros_half, qt_ref[0, 0, base + HEAD_DIM:base + pair, :]], axis=0)

    ones_rows = jnp.ones((ONES_ROWS, blk), _BF16)
    key_pos = lax.broadcasted_iota(jnp.int32, (blk, blk), 0)
    qry_pos = lax.broadcasted_iota(jnp.int32, (blk, blk), 1)
    causal = key_pos <= qry_pos

    def blocks_step(js, own):
        k_pairs = []
        for j in js:
            row0 = pl.multiple_of(j * blk, blk)
            k_pairs.append([k_ref[pl.ds(row0, blk), p * pair:(p + 1) * pair] for p in range(n_heads // 2)])
        col_max = {}

        def issue_scores(h):
            for t in range(len(js)):
                s = _dot(k_pairs[t][h // 2], qm_ref[h])
                if own:
                    s = jnp.where(causal, s, MASK_NEG)
                s_ref[t, h] = s
                col_max[t, h] = jnp.max(s, axis=0, keepdims=True)

        ahead = min(QK_LOOKAHEAD[len(js)], n_heads)
        for h in range(ahead):
            issue_scores(h)
        for h in range(n_heads):
            if h + ahead < n_heads:
                issue_scores(h + ahead)
            bs = [None if own else bias_ref[h, j] for j in js]
            m_old = None if own else m_ref[h]
            m_new = m_old
            for t, b in enumerate(bs):
                cand = col_max[t, h] if b is None else col_max[t, h] + b
                m_new = cand if m_new is None else jnp.maximum(m_new, cand)
            pv = None
            for t, j in enumerate(js):
                p = jnp.exp2(s_ref[t, h] - m_new).astype(_BF16)
                vt = jnp.concatenate([vt_ref[0, j, h * HEAD_DIM:(h + 1) * HEAD_DIM, :], ones_rows], axis=0)
                c = _dot(vt, p)
                if bs[t] is not None:
                    c = jnp.where(bs[t] == 0.0, c, 0.0)
                pv = c if pv is None else pv + c
            if own:
                acc_ref[h] = pv
            else:
                acc_ref[h] = jnp.exp2(m_old - m_new) * acc_ref[h] + pv
            m_ref[h] = m_new

    def gate_and_lead(n_lead_blocks):
        blk_id = lax.broadcasted_iota(jnp.int32, (n_blocks, blk), 0)
        valid = blk_id < i
        gates = []
        for h in range(n_heads):
            base = (h // 2) * pair
            km = km_ref[0, :, base:base + pair]
            km_hi = km.astype(_BF16)
            km_lo = (km - km_hi.astype(_F32)).astype(_BF16)
            qm = qm_ref[h]
            gates.append(_dot(km_hi, qm) + _dot(km_lo, qm))

        blocks_step([i], True)

        for h in range(n_heads):
            g = jnp.where(valid, gates[h], -jnp.inf)
            picked = jnp.zeros((n_blocks, blk), _F32)
            for _ in range(MOBA_TOPK):
                g_max = jnp.max(g, axis=0, keepdims=True)
                first = jnp.min(jnp.where(g == g_max, blk_id, n_blocks), axis=0, keepdims=True)
                hit = blk_id == first
                picked = jnp.where(hit, 1.0, picked)
                g = jnp.where(hit, -jnp.inf, g)
            bias = jnp.where(valid, jnp.where(picked > 0.0, 0.0, MASK_NEG), MASK_NEG)
            for j in range(n_blocks):
                bias_ref[h, j] = bias[j:j + 1, :]

        start = 0
        while start < n_lead_blocks:
            size = (n_lead_blocks - start - 1) % BLOCKS_PER_ITER + 1
            blocks_step(list(range(start, start + size)), False)
            start += size

    n_lead = jnp.where(i >= BLOCKS_PER_ITER, i % BLOCKS_PER_ITER + BLOCKS_PER_ITER, i)
    for r in range(2 * BLOCKS_PER_ITER):
        @pl.when(n_lead == r)
        def _(r=r):
            gate_and_lead(r)

    n_iter = (i - n_lead) // BLOCKS_PER_ITER
    for r in range(BLOCKS_PER_ITER, 2 * BLOCKS_PER_ITER):
        @pl.when((n_lead == r) & (n_iter > 0))
        def _(r=r):
            blocks_step(list(range(r, r + BLOCKS_PER_ITER)), False)

    def body(u, carry):
        blocks_step([n_lead + BLOCKS_PER_ITER * u + t for t in range(BLOCKS_PER_ITER)], False)
        return carry

    lax.fori_loop(1, n_iter, body, 0)

    outs = []
    for h in range(n_heads):
        acc = acc_ref[h]
        outs.append(acc[0:HEAD_DIM] * (1.0 / acc[HEAD_DIM:HEAD_DIM + 1]))
    ot = jnp.concatenate(outs, axis=0)
    o_ref[0, 0] = (ot * ga_ref[0, 0].astype(_F32)).astype(_BF16)


def _out_kernel(yp_ref, yat_ref, x_ref, w_ref, gain_ref, bias_ref, o_ref, *, d_pool, alpha):
    rows = MOBA_BLOCK
    for c in range(x_ref.shape[0] // rows):
        r = slice(c * rows, (c + 1) * rows)
        out = _dot(yp_ref[r, :], w_ref[0:d_pool, :]) + _dot_tn(yat_ref[0, c], w_ref[d_pool:, :])
        z = alpha * x_ref[r, :] + out
        mu = jnp.mean(z, axis=-1, keepdims=True)
        zc = z - mu
        var = jnp.mean(zc * zc, axis=-1, keepdims=True)
        o_ref[r, :] = zc * lax.rsqrt(var + LN_EPS) * gain_ref[...] + bias_ref[...]


def _layer(h, positions, w_in, pool_w, pool_scale, w_out, ln_gain, ln_bias, alpha):
    b, s, d_model = h.shape
    d_pool = pool_scale.shape[0]
    d_attn = d_model - d_pool
    n_heads = d_attn // HEAD_DIM
    n_blocks = s // MOBA_BLOCK
    tm = ROW_TILE
    n_tiles = s // tm
    rows = b * s
    blocks_per_tile = tm // MOBA_BLOCK
    assert s % tm == 0 and tm % MOBA_BLOCK == 0 and d_attn % LANES == 0
    assert w_in.shape == (d_model, 2 * d_pool + 4 * d_attn)

    c_q = 2 * d_pool
    w_std = w_in[:, :c_q].astype(_BF16)
    w_t = w_in[:, c_q:].T.astype(_BF16)
    freqs = ROPE_THETA ** (-jnp.arange(ROT_HALF, dtype=_F32) * 2.0 / ROT_DIM)
    freq_tab = jnp.broadcast_to(freqs[:, None], (ROT_HALF, tm))
    x2 = h.reshape(rows, d_model)
    pos3 = positions.reshape(b, 1, s)

    proj = pl.pallas_call(
        functools.partial(_proj_kernel, d_pool=d_pool, d_attn=d_attn, n_heads=n_heads, tm=tm),
        grid=(b, n_tiles),
        in_specs=[
            pl.BlockSpec((tm, d_model), lambda bi, si: (bi * n_tiles + si, 0)),
            pl.BlockSpec((1, 1, tm), lambda bi, si: (bi, 0, si)),
            pl.BlockSpec((ROT_HALF, tm), lambda bi, si: (0, 0)),
            pl.BlockSpec(w_std.shape, lambda bi, si: (0, 0)),
            pl.BlockSpec(w_t.shape, lambda bi, si: (0, 0)),
            pl.BlockSpec(pool_w.shape, lambda bi, si: (0, 0, 0)),
            pl.BlockSpec((1, d_pool), lambda bi, si: (0, 0)),
        ],
        out_specs=[
            pl.BlockSpec((tm, d_pool), lambda bi, si: (bi * n_tiles + si, 0)),
            pl.BlockSpec((1, blocks_per_tile, d_attn, MOBA_BLOCK), lambda bi, si: (bi, si, 0, 0)),
            pl.BlockSpec((tm, d_attn), lambda bi, si: (bi * n_tiles + si, 0)),
            pl.BlockSpec((1, blocks_per_tile, d_attn, MOBA_BLOCK), lambda bi, si: (bi, si, 0, 0)),
            pl.BlockSpec((blocks_per_tile, 1, d_attn), lambda bi, si: (bi * n_tiles + si, 0, 0)),
            pl.BlockSpec((1, blocks_per_tile, d_attn, MOBA_BLOCK), lambda bi, si: (bi, si, 0, 0)),
        ],
        out_shape=[
            jax.ShapeDtypeStruct((rows, d_pool), _BF16),
            jax.ShapeDtypeStruct((b, n_blocks, d_attn, MOBA_BLOCK), _BF16),
            jax.ShapeDtypeStruct((rows, d_attn), _BF16),
            jax.ShapeDtypeStruct((b, n_blocks, d_attn, MOBA_BLOCK), _BF16),
            jax.ShapeDtypeStruct((b * n_blocks, 1, d_attn), _F32),
            jax.ShapeDtypeStruct((b, n_blocks, d_attn, MOBA_BLOCK), _BF16),
        ],
        scratch_shapes=[pltpu.VMEM((tm + POOL_HALO, d_pool), _F32)],
        compiler_params=pltpu.CompilerParams(
            dimension_semantics=("arbitrary", "arbitrary"), vmem_limit_bytes=VMEM_LIMIT),
        name="proj_pool_rope",
    )
    y_pool, q_t, k_rot, v_t, k_mean, g_attn = proj(
        x2, pos3, freq_tab, w_std, w_t, pool_w.astype(_BF16), pool_scale.reshape(1, d_pool))

    k_mean = k_mean.reshape(b, n_blocks, d_attn)
    attn = pl.pallas_call(
        functools.partial(_attn_kernel, n_blocks=n_blocks, n_heads=n_heads),
        grid=(b, n_blocks),
        in_specs=[
            pl.BlockSpec((1, 1, d_attn, MOBA_BLOCK), lambda bi, i: (bi, i, 0, 0)),
            pl.BlockSpec((s, d_attn), lambda bi, i: (bi, 0)),
            pl.BlockSpec((1, n_blocks, d_attn, MOBA_BLOCK), lambda bi, i: (bi, 0, 0, 0)),
            pl.BlockSpec((1, n_blocks, d_attn), lambda bi, i: (bi, 0, 0)),
            pl.BlockSpec((1, 1, d_attn, MOBA_BLOCK), lambda bi, i: (bi, i, 0, 0)),
        ],
        out_specs=pl.BlockSpec((1, 1, d_attn, MOBA_BLOCK), lambda bi, i: (bi, i, 0, 0)),
        out_shape=jax.ShapeDtypeStruct((b, n_blocks, d_attn, MOBA_BLOCK), _BF16),
        scratch_shapes=[
            pltpu.VMEM((n_heads, 2 * HEAD_DIM, MOBA_BLOCK), _BF16),
            pltpu.VMEM((n_heads, n_blocks, 1, MOBA_BLOCK), _F32),
            pltpu.VMEM((n_heads, 1, MOBA_BLOCK), _F32),
            pltpu.VMEM((n_heads, HEAD_DIM + ONES_ROWS, MOBA_BLOCK), _F32),
            pltpu.VMEM((BLOCKS_PER_ITER, n_heads, MOBA_BLOCK, MOBA_BLOCK), _F32),
        ],
        compiler_params=pltpu.CompilerParams(
            dimension_semantics=("arbitrary", "arbitrary"), vmem_limit_bytes=VMEM_LIMIT),
        name="moba_attention",
    )
    y_attn = attn(q_t, k_rot, v_t, k_mean, g_attn)

    out_blocks = OUT_ROW_TILE // MOBA_BLOCK
    out_tiles_per_seq = s // OUT_ROW_TILE
    assert s % OUT_ROW_TILE == 0 and OUT_ROW_TILE % MOBA_BLOCK == 0
    outp = pl.pallas_call(
        functools.partial(_out_kernel, d_pool=d_pool, alpha=alpha),
        grid=(rows // OUT_ROW_TILE,),
        in_specs=[
            pl.BlockSpec((OUT_ROW_TILE, d_pool), lambda r: (r, 0)),
            pl.BlockSpec((1, out_blocks, d_attn, MOBA_BLOCK),
                         lambda r: (r // out_tiles_per_seq, r % out_tiles_per_seq, 0, 0)),
            pl.BlockSpec((OUT_ROW_TILE, d_model), lambda r: (r, 0)),
            pl.BlockSpec((d_model, d_model), lambda r: (0, 0)),
            pl.BlockSpec((1, d_model), lambda r: (0, 0)),
            pl.BlockSpec((1, d_model), lambda r: (0, 0)),
        ],
        out_specs=pl.BlockSpec((OUT_ROW_TILE, d_model), lambda r: (r, 0)),
        out_shape=jax.ShapeDtypeStruct((rows, d_model), h.dtype),
        compiler_params=pltpu.CompilerParams(
            dimension_semantics=("arbitrary",), vmem_limit_bytes=VMEM_LIMIT),
        name="out_proj_layernorm",
    )
    out = outp(y_pool, y_attn, x2, w_out.astype(_BF16),
               ln_gain.reshape(1, d_model), ln_bias.reshape(1, d_model))
    return out.reshape(b, s, d_model)


def kernel(x, positions, w_in, pool_w, pool_scale, w_out, ln_gain, ln_bias):
    depth = w_in.shape[0]
    alpha = (2.0 * depth) ** 0.25
    h = x
    for layer in range(depth):
        h = _layer(h, positions, w_in[layer], pool_w[layer], pool_scale[layer],
                   w_out[layer], ln_gain[layer], ln_bias[layer], alpha)
    return h
```

```python
import functools

import jax
import jax.numpy as jnp
from jax import lax
from jax.experimental import pallas as pl
from jax.experimental.pallas import tpu as pltpu

POOL_WINDOWS = (2, 4, 8, 16)
HEAD_DIM = 64
ROT_DIM = HEAD_DIM // 4
ROT_HALF = ROT_DIM // 2
ROPE_THETA = 500000.0
MOBA_BLOCK = 256
MOBA_TOPK = 3
LN_EPS = 1e-5

LANES = 128
POOL_HALO = 16
ROW_TILE = 1024
MASK_NEG = -1e30
ONES_ROWS = 16
LOG2_E = 1.4426950408889634
BLOCKS_PER_ITER = 4
QK_LOOKAHEAD = {1: 8, 2: 3, 3: 3, 4: 2}
OUT_ROW_TILE = 1024
VMEM_LIMIT = 48 * 1024 * 1024

_F32 = jnp.float32
_BF16 = jnp.bfloat16


def _silu(x):
    half = 0.5 * x
    return half + half * jnp.tanh(half)


def _dot(a, b):
    return jnp.dot(a, b, preferred_element_type=_F32)


def _dot_tt(a, b):
    return lax.dot_general(a, b, (((0,), (1,)), ((), ())), preferred_element_type=_F32)


def _dot_tn(a, b):
    return lax.dot_general(a, b, (((0,), (0,)), ((), ())), preferred_element_type=_F32)


def _rotate_heads(t, cos, sin, n_heads):
    pieces = []
    for h in range(n_heads):
        base = h * HEAD_DIM
        x1 = t[base:base + ROT_HALF]
        x2 = t[base + ROT_HALF:base + ROT_DIM]
        pieces.append(x1 * cos - x2 * sin)
        pieces.append(x2 * cos + x1 * sin)
        pieces.append(t[base + ROT_DIM:base + HEAD_DIM])
    return jnp.concatenate(pieces, axis=0)


def _proj_kernel(x_ref, pos_ref, freq_ref, w_ref, poolw_ref, pscale_ref,
                 ypool_ref, qt_ref, k_ref, vt_ref, km_ref, gat_ref, ubuf_ref,
                 *, d_pool, d_attn, n_heads, tm):
    s_idx = pl.program_id(1)
    c_q = 2 * d_pool

    @pl.when((pl.program_id(0) == 0) & (s_idx == 0))
    def _():
        ubuf_ref[0:POOL_HALO, :] = jnp.zeros((POOL_HALO, d_pool), _F32)

    xb = x_ref[...].astype(_BF16)

    kt_raw = _dot_tt(w_ref[:, c_q + d_attn:c_q + 2 * d_attn], xb)
    std = _dot(xb, w_ref[:, 0:c_q])
    qt_raw = _dot_tt(w_ref[:, c_q:c_q + d_attn], xb)
    gat_raw = _dot_tt(w_ref[:, c_q + 3 * d_attn:c_q + 4 * d_attn], xb)
    vt_raw = _dot_tt(w_ref[:, c_q + 2 * d_attn:c_q + 3 * d_attn], xb)

    ubuf_ref[POOL_HALO:POOL_HALO + tm, :] = std[:, 0:d_pool]
    group = d_pool // len(POOL_WINDOWS)
    t_head = s_idx * tm + lax.broadcasted_iota(jnp.int32, (POOL_HALO, group), 0)
    for g, w in enumerate(POOL_WINDOWS):
        cols = slice(g * group, (g + 1) * group)
        u_g = ubuf_ref[POOL_HALO:POOL_HALO + tm, cols]
        tsum = u_g
        for s in range(1, w):
            tsum = tsum + ubuf_ref[POOL_HALO - s:POOL_HALO - s + tm, cols]
        count_head = jnp.minimum(t_head + 1, w).astype(_F32)
        mean_head = tsum[0:POOL_HALO] / count_head
        mean = jnp.concatenate([mean_head, tsum[POOL_HALO:] * (1.0 / w)], axis=0)
        d = mean - u_g
        y = _dot(d.astype(_BF16), poolw_ref[g])
        gate = _silu(std[:, d_pool + g * group:d_pool + (g + 1) * group])
        ypool_ref[:, cols] = (y * pscale_ref[:, cols] * gate).astype(_BF16)
    last_tile = s_idx == pl.num_programs(1) - 1
    ubuf_ref[0:POOL_HALO, :] = jnp.where(last_tile, 0.0, ubuf_ref[tm:tm + POOL_HALO, :])

    gat = _silu(gat_raw).astype(_BF16)
    for r in range(tm // MOBA_BLOCK):
        gat_ref[0, r] = gat[:, r * MOBA_BLOCK:(r + 1) * MOBA_BLOCK]

    ang = freq_ref[...] * pos_ref[0].astype(_F32)
    cos = jnp.cos(ang)
    sin = jnp.sin(ang)

    scale = HEAD_DIM ** -0.5 * LOG2_E
    qt = _rotate_heads(qt_raw, cos, sin, n_heads) * scale
    qt = qt.astype(_BF16)
    for r in range(tm // MOBA_BLOCK):
        qt_ref[0, r] = qt[:, r * MOBA_BLOCK:(r + 1) * MOBA_BLOCK]

    kt = _rotate_heads(kt_raw, cos, sin, n_heads)
    for c in range(d_attn // LANES):
        kc = kt[c * LANES:(c + 1) * LANES].T
        k_ref[:, c * LANES:(c + 1) * LANES] = kc.astype(_BF16)
        for r in range(tm // MOBA_BLOCK):
            blk_sum = jnp.sum(kc[r * MOBA_BLOCK:(r + 1) * MOBA_BLOCK], axis=0, keepdims=True)
            km_ref[r, :, c * LANES:(c + 1) * LANES] = blk_sum * (1.0 / MOBA_BLOCK)

    vt = vt_raw.astype(_BF16)
    for r in range(tm // MOBA_BLOCK):
        vt_ref[0, r] = vt[:, r * MOBA_BLOCK:(r + 1) * MOBA_BLOCK]


def _attn_kernel(qt_ref, k_ref, vt_ref, km_ref, ga_ref, o_ref,
                 qm_ref, bias_ref, m_ref, acc_ref, s_ref, *, n_blocks, n_heads):
    i = pl.program_id(1)
    blk = MOBA_BLOCK
    pair = 2 * HEAD_DIM

    zeros_half = jnp.zeros((HEAD_DIM, blk), _BF16)
    for h in range(n_heads):
        base = (h // 2) * pair
        if h % 2 == 0:
            qm_ref[h] = jnp.concatenate([qt_ref[0, 0, base:base + HEAD_DIM, :], zeros_half], axis=0)
        else:
            qm_ref[h] = jnp.concatenate([zeros_half, qt_ref[0, 0, base + HEAD_DIM:base + pair, :]], axis=0)

    ones_rows = jnp.ones((ONES_ROWS, blk), _BF16)
    key_pos = lax.broadcasted_iota(jnp.int32, (blk, blk), 0)
    qry_pos = lax.broadcasted_iota(jnp.int32, (blk, blk), 1)
    causal = key_pos <= qry_pos

    def blocks_step(js, own):
        k_pairs = []
        for j in js:
            row0 = pl.multiple_of(j * blk, blk)
            k_pairs.append([k_ref[pl.ds(row0, blk), p * pair:(p + 1) * pair] for p in range(n_heads // 2)])
        col_max = {}

        def issue_scores(h):
            for t in range(len(js)):
                s = _dot(k_pairs[t][h // 2], qm_ref[h])
                if own:
                    s = jnp.where(causal, s, MASK_NEG)
                s_ref[t, h] = s
                col_max[t, h] = jnp.max(s, axis=0, keepdims=True)

        ahead = min(QK_LOOKAHEAD[len(js)], n_heads)
        for h in range(ahead):
            issue_scores(h)
        for h in range(n_heads):
            if h + ahead < n_heads:
                issue_scores(h + ahead)
            bs = [None if own else bias_ref[h, j] for j in js]
            m_old = None if own else m_ref[h]
            m_new = m_old
            for t, b in enumerate(bs):
                cand = col_max[t, h] if b is None else col_max[t, h] + b
                m_new = cand if m_new is None else jnp.maximum(m_new, cand)
            pv = None
            for t, j in enumerate(js):
                ref_point = m_new if bs[t] is None else m_new - bs[t]
                p = jnp.exp2(s_ref[t, h] - ref_point).astype(_BF16)
                vt = jnp.concatenate([vt_ref[0, j, h * HEAD_DIM:(h + 1) * HEAD_DIM, :], ones_rows], axis=0)
                c = _dot(vt, p)
                pv = c if pv is None else pv + c
            if own:
                acc_ref[h] = pv
            else:
                acc_ref[h] = jnp.exp2(m_old - m_new) * acc_ref[h] + pv
            m_ref[h] = m_new

    def gate_and_lead(n_lead_blocks):
        blk_id = lax.broadcasted_iota(jnp.int32, (n_blocks, blk), 0)
        valid = blk_id < i
        gates = []
        for h in range(n_heads):
            base = (h // 2) * pair
            km = km_ref[0, :, base:base + pair]
            km_hi = km.astype(_BF16)
            km_lo = (km - km_hi.astype(_F32)).astype(_BF16)
            qm = qm_ref[h]
            gates.append(_dot(km_hi, qm) + _dot(km_lo, qm))

        blocks_step([i], True)

        for h in range(n_heads):
            g = jnp.where(valid, gates[h], -jnp.inf)
            picked = jnp.zeros((n_blocks, blk), _F32)
            for _ in range(MOBA_TOPK):
                g_max = jnp.max(g, axis=0, keepdims=True)
                first = jnp.min(jnp.where(g == g_max, blk_id, n_blocks), axis=0, keepdims=True)
                hit = blk_id == first
                picked = jnp.where(hit, 1.0, picked)
                g = jnp.where(hit, -jnp.inf, g)
            bias = jnp.where(valid, jnp.where(picked > 0.0, 0.0, MASK_NEG), MASK_NEG)
            for j in range(n_blocks):
                bias_ref[h, j] = bias[j:j + 1, :]

        start = 0
        while start < n_lead_blocks:
            size = (n_lead_blocks - start - 1) % BLOCKS_PER_ITER + 1
            blocks_step(list(range(start, start + size)), False)
            start += size

    n_lead = jnp.where(i >= BLOCKS_PER_ITER, i % BLOCKS_PER_ITER + BLOCKS_PER_ITER, i)
    for r in range(2 * BLOCKS_PER_ITER):
        @pl.when(n_lead == r)
        def _(r=r):
            gate_and_lead(r)

    def body(u, carry):
        blocks_step([n_lead + BLOCKS_PER_ITER * u + t for t in range(BLOCKS_PER_ITER)], False)
        return carry

    lax.fori_loop(0, (i - n_lead) // BLOCKS_PER_ITER, body, 0)

    outs = []
    for h in range(n_heads):
        acc = acc_ref[h]
        outs.append(acc[0:HEAD_DIM] * (1.0 / acc[HEAD_DIM:HEAD_DIM + 1]))
    ot = jnp.concatenate(outs, axis=0)
    o_ref[0, 0] = (ot * ga_ref[0, 0].astype(_F32)).astype(_BF16)


def _out_kernel(yp_ref, yat_ref, x_ref, w_ref, gain_ref, bias_ref, o_ref, *, d_pool, alpha):
    rows = MOBA_BLOCK
    for c in range(x_ref.shape[0] // rows):
        r = slice(c * rows, (c + 1) * rows)
        out = _dot(yp_ref[r, :], w_ref[0:d_pool, :]) + _dot_tn(yat_ref[0, c], w_ref[d_pool:, :])
        z = alpha * x_ref[r, :] + out
        mu = jnp.mean(z, axis=-1, keepdims=True)
        zc = z - mu
        var = jnp.mean(zc * zc, axis=-1, keepdims=True)
        o_ref[r, :] = zc * lax.rsqrt(var + LN_EPS) * gain_ref[...] + bias_ref[...]


def _layer(h, positions, w_in, pool_w, pool_scale, w_out, ln_gain, ln_bias, alpha):
    b, s, d_model = h.shape
    d_pool = pool_scale.shape[0]
    d_attn = d_model - d_pool
    n_heads = d_attn // HEAD_DIM
    n_blocks = s // MOBA_BLOCK
    tm = ROW_TILE
    n_tiles = s // tm
    rows = b * s
    blocks_per_tile = tm // MOBA_BLOCK
    assert s % tm == 0 and tm % MOBA_BLOCK == 0 and d_attn % LANES == 0
    assert w_in.shape == (d_model, 2 * d_pool + 4 * d_attn)

    w_bf = w_in.astype(_BF16)
    freqs = ROPE_THETA ** (-jnp.arange(ROT_HALF, dtype=_F32) * 2.0 / ROT_DIM)
    freq_tab = jnp.broadcast_to(freqs[:, None], (ROT_HALF, tm))
    x2 = h.reshape(rows, d_model)
    pos3 = positions.reshape(b, 1, s)

    proj = pl.pallas_call(
        functools.partial(_proj_kernel, d_pool=d_pool, d_attn=d_attn, n_heads=n_heads, tm=tm),
        grid=(b, n_tiles),
        in_specs=[
            pl.BlockSpec((tm, d_model), lambda bi, si: (bi * n_tiles + si, 0)),
            pl.BlockSpec((1, 1, tm), lambda bi, si: (bi, 0, si)),
            pl.BlockSpec((ROT_HALF, tm), lambda bi, si: (0, 0)),
            pl.BlockSpec(w_bf.shape, lambda bi, si: (0, 0)),
            pl.BlockSpec(pool_w.shape, lambda bi, si: (0, 0, 0)),
            pl.BlockSpec((1, d_pool), lambda bi, si: (0, 0)),
        ],
        out_specs=[
            pl.BlockSpec((tm, d_pool), lambda bi, si: (bi * n_tiles + si, 0)),
            pl.BlockSpec((1, blocks_per_tile, d_attn, MOBA_BLOCK), lambda bi, si: (bi, si, 0, 0)),
            pl.BlockSpec((tm, d_attn), lambda bi, si: (bi * n_tiles + si, 0)),
            pl.BlockSpec((1, blocks_per_tile, d_attn, MOBA_BLOCK), lambda bi, si: (bi, si, 0, 0)),
            pl.BlockSpec((blocks_per_tile, 1, d_attn), lambda bi, si: (bi * n_tiles + si, 0, 0)),
            pl.BlockSpec((1, blocks_per_tile, d_attn, MOBA_BLOCK), lambda bi, si: (bi, si, 0, 0)),
        ],
        out_shape=[
            jax.ShapeDtypeStruct((rows, d_pool), _BF16),
            jax.ShapeDtypeStruct((b, n_blocks, d_attn, MOBA_BLOCK), _BF16),
            jax.ShapeDtypeStruct((rows, d_attn), _BF16),
            jax.ShapeDtypeStruct((b, n_blocks, d_attn, MOBA_BLOCK), _BF16),
            jax.ShapeDtypeStruct((b * n_blocks, 1, d_attn), _F32),
            jax.ShapeDtypeStruct((b, n_blocks, d_attn, MOBA_BLOCK), _BF16),
        ],
        scratch_shapes=[pltpu.VMEM((tm + POOL_HALO, d_pool), _F32)],
        compiler_params=pltpu.CompilerParams(
            dimension_semantics=("arbitrary", "arbitrary"), vmem_limit_bytes=VMEM_LIMIT),
        name="proj_pool_rope",
    )
    y_pool, q_t, k_rot, v_t, k_mean, g_attn = proj(
        x2, pos3, freq_tab, w_bf, pool_w.astype(_BF16), pool_scale.reshape(1, d_pool))

    k_mean = k_mean.reshape(b, n_blocks, d_attn)
    attn = pl.pallas_call(
        functools.partial(_attn_kernel, n_blocks=n_blocks, n_heads=n_heads),
        grid=(b, n_blocks),
        in_specs=[
            pl.BlockSpec((1, 1, d_attn, MOBA_BLOCK), lambda bi, i: (bi, i, 0, 0)),
            pl.BlockSpec((s, d_attn), lambda bi, i: (bi, 0)),
            pl.BlockSpec((1, n_blocks, d_attn, MOBA_BLOCK), lambda bi, i: (bi, 0, 0, 0)),
            pl.BlockSpec((1, n_blocks, d_attn), lambda bi, i: (bi, 0, 0)),
            pl.BlockSpec((1, 1, d_attn, MOBA_BLOCK), lambda bi, i: (bi, i, 0, 0)),
        ],
        out_specs=pl.BlockSpec((1, 1, d_attn, MOBA_BLOCK), lambda bi, i: (bi, i, 0, 0)),
        out_shape=jax.ShapeDtypeStruct((b, n_blocks, d_attn, MOBA_BLOCK), _BF16),
        scratch_shapes=[
            pltpu.VMEM((n_heads, 2 * HEAD_DIM, MOBA_BLOCK), _BF16),
            pltpu.VMEM((n_heads, n_blocks, 1, MOBA_BLOCK), _F32),
            pltpu.VMEM((n_heads, 1, MOBA_BLOCK), _F32),
            pltpu.VMEM((n_heads, HEAD_DIM + ONES_ROWS, MOBA_BLOCK), _F32),
            pltpu.VMEM((BLOCKS_PER_ITER, n_heads, MOBA_BLOCK, MOBA_BLOCK), _F32),
        ],
        compiler_params=pltpu.CompilerParams(
            dimension_semantics=("arbitrary", "arbitrary"), vmem_limit_bytes=VMEM_LIMIT),
        name="moba_attention",
    )
    y_attn = attn(q_t, k_rot, v_t, k_mean, g_attn)

    out_blocks = OUT_ROW_TILE // MOBA_BLOCK
    out_tiles_per_seq = s // OUT_ROW_TILE
    assert s % OUT_ROW_TILE == 0 and OUT_ROW_TILE % MOBA_BLOCK == 0
    outp = pl.pallas_call(
        functools.partial(_out_kernel, d_pool=d_pool, alpha=alpha),
        grid=(rows // OUT_ROW_TILE,),
        in_specs=[
            pl.BlockSpec((OUT_ROW_TILE, d_pool), lambda r: (r, 0)),
            pl.BlockSpec((1, out_blocks, d_attn, MOBA_BLOCK),
                         lambda r: (r // out_tiles_per_seq, r % out_tiles_per_seq, 0, 0)),
            pl.BlockSpec((OUT_ROW_TILE, d_model), lambda r: (r, 0)),
            pl.BlockSpec((d_model, d_model), lambda r: (0, 0)),
            pl.BlockSpec((1, d_model), lambda r: (0, 0)),
            pl.BlockSpec((1, d_model), lambda r: (0, 0)),
        ],
        out_specs=pl.BlockSpec((OUT_ROW_TILE, d_model), lambda r: (r, 0)),
        out_shape=jax.ShapeDtypeStruct((rows, d_model), h.dtype),
        compiler_params=pltpu.CompilerParams(
            dimension_semantics=("arbitrary",), vmem_limit_bytes=VMEM_LIMIT),
        name="out_proj_layernorm",
    )
    out = outp(y_pool, y_attn, x2, w_out.astype(_BF16),
               ln_gain.reshape(1, d_model), ln_bias.reshape(1, d_model))
    return out.reshape(b, s, d_model)


def kernel(x, positions, w_in, pool_w, pool_scale, w_out, ln_gain, ln_bias):
    depth = w_in.shape[0]
    alpha = (2.0 * depth) ** 0.25
    h = x
    for layer in range(depth):
        h = _layer(h, positions, w_in[layer], pool_w[layer], pool_scale[layer],
                   w_out[layer], ln_gain[layer], ln_bias[layer], alpha)
    return h
```

```python
import functools

import jax
import jax.numpy as jnp
from jax import lax
from jax.experimental import pallas as pl
from jax.experimental.pallas import tpu as pltpu

POOL_WINDOWS = (2, 4, 8, 16)
HEAD_DIM = 64
ROT_DIM = HEAD_DIM // 4
ROT_HALF = ROT_DIM // 2
ROPE_THETA = 500000.0
MOBA_BLOCK = 256
MOBA_TOPK = 3
LN_EPS = 1e-5

LANES = 128
POOL_HALO = 16
ROW_TILE = 1024
MASK_NEG = -1e30
ONES_ROWS = 16
LOG2_E = 1.4426950408889634
BLOCKS_PER_ITER = 4
QK_LOOKAHEAD = {1: 8, 2: 3, 3: 3, 4: 2}
OUT_ROW_TILE = 1024
LN_ROWS = 32
VMEM_LIMIT = 48 * 1024 * 1024

_F32 = jnp.float32
_BF16 = jnp.bfloat16


def _silu(x):
    half = 0.5 * x
    return half + half * jnp.tanh(half)


def _dot(a, b):
    return jnp.dot(a, b, preferred_element_type=_F32)


def _dot_tt(a, b):
    return lax.dot_general(a, b, (((0,), (1,)), ((), ())), preferred_element_type=_F32)


def _dot_tn(a, b):
    return lax.dot_general(a, b, (((0,), (0,)), ((), ())), preferred_element_type=_F32)


def _rotate_heads(t, cos, sin, n_heads):
    pieces = []
    for h in range(n_heads):
        base = h * HEAD_DIM
        x1 = t[base:base + ROT_HALF]
        x2 = t[base + ROT_HALF:base + ROT_DIM]
        pieces.append(x1 * cos - x2 * sin)
        pieces.append(x2 * cos + x1 * sin)
        pieces.append(t[base + ROT_DIM:base + HEAD_DIM])
    return jnp.concatenate(pieces, axis=0)


def _proj_kernel(x_ref, pos_ref, freq_ref, w_ref, poolw_ref, pscale_ref,
                 ypool_ref, qt_ref, k_ref, vt_ref, km_ref, gat_ref, ubuf_ref,
                 *, d_pool, d_attn, n_heads, tm):
    s_idx = pl.program_id(1)
    c_q = 2 * d_pool

    @pl.when((pl.program_id(0) == 0) & (s_idx == 0))
    def _():
        ubuf_ref[0:POOL_HALO, :] = jnp.zeros((POOL_HALO, d_pool), _F32)

    xb = x_ref[...].astype(_BF16)

    kt_raw = _dot_tt(w_ref[:, c_q + d_attn:c_q + 2 * d_attn], xb)
    std = _dot(xb, w_ref[:, 0:c_q])
    qt_raw = _dot_tt(w_ref[:, c_q:c_q + d_attn], xb)
    gat_raw = _dot_tt(w_ref[:, c_q + 3 * d_attn:c_q + 4 * d_attn], xb)
    vt_raw = _dot_tt(w_ref[:, c_q + 2 * d_attn:c_q + 3 * d_attn], xb)

    ubuf_ref[POOL_HALO:POOL_HALO + tm, :] = std[:, 0:d_pool]
    group = d_pool // len(POOL_WINDOWS)
    t_head = s_idx * tm + lax.broadcasted_iota(jnp.int32, (POOL_HALO, group), 0)
    for g, w in enumerate(POOL_WINDOWS):
        cols = slice(g * group, (g + 1) * group)
        u_g = ubuf_ref[POOL_HALO:POOL_HALO + tm, cols]
        tsum = u_g
        for s in range(1, w):
            tsum = tsum + ubuf_ref[POOL_HALO - s:POOL_HALO - s + tm, cols]
        count_head = jnp.minimum(t_head + 1, w).astype(_F32)
        mean_head = tsum[0:POOL_HALO] / count_head
        mean = jnp.concatenate([mean_head, tsum[POOL_HALO:] * (1.0 / w)], axis=0)
        d = mean - u_g
        y = _dot(d.astype(_BF16), poolw_ref[g])
        gate = _silu(std[:, d_pool + g * group:d_pool + (g + 1) * group])
        ypool_ref[:, cols] = (y * pscale_ref[:, cols] * gate).astype(_BF16)
    last_tile = s_idx == pl.num_programs(1) - 1
    ubuf_ref[0:POOL_HALO, :] = jnp.where(last_tile, 0.0, ubuf_ref[tm:tm + POOL_HALO, :])

    gat = _silu(gat_raw).astype(_BF16)
    for r in range(tm // MOBA_BLOCK):
        gat_ref[0, r] = gat[:, r * MOBA_BLOCK:(r + 1) * MOBA_BLOCK]

    ang = freq_ref[...] * pos_ref[0].astype(_F32)
    cos = jnp.cos(ang)
    sin = jnp.sin(ang)

    scale = HEAD_DIM ** -0.5 * LOG2_E
    qt = _rotate_heads(qt_raw, cos, sin, n_heads) * scale
    qt = qt.astype(_BF16)
    for r in range(tm // MOBA_BLOCK):
        qt_ref[0, r] = qt[:, r * MOBA_BLOCK:(r + 1) * MOBA_BLOCK]

    kt = _rotate_heads(kt_raw, cos, sin, n_heads)
    for c in range(d_attn // LANES):
        kc = kt[c * LANES:(c + 1) * LANES].T
        k_ref[:, c * LANES:(c + 1) * LANES] = kc.astype(_BF16)
        for r in range(tm // MOBA_BLOCK):
            blk_sum = jnp.sum(kc[r * MOBA_BLOCK:(r + 1) * MOBA_BLOCK], axis=0, keepdims=True)
            km_ref[r, :, c * LANES:(c + 1) * LANES] = blk_sum * (1.0 / MOBA_BLOCK)

    vt = vt_raw.astype(_BF16)
    for r in range(tm // MOBA_BLOCK):
        vt_ref[0, r] = vt[:, r * MOBA_BLOCK:(r + 1) * MOBA_BLOCK]


def _attn_kernel(qt_ref, k_ref, vt_ref, km_ref, ga_ref, o_ref,
                 qm_ref, bias_ref, m_ref, acc_ref, s_ref, *, n_blocks, n_heads):
    i = pl.program_id(1)
    blk = MOBA_BLOCK
    pair = 2 * HEAD_DIM

    zeros_half = jnp.zeros((HEAD_DIM, blk), _BF16)
    for h in range(n_heads):
        base = (h // 2) * pair
        if h % 2 == 0:
            qm_ref[h] = jnp.concatenate([qt_ref[0, 0, base:base + HEAD_DIM, :], zeros_half], axis=0)
        else:
            qm_ref[h] = jnp.concatenate([zeros_half, qt_ref[0, 0, base + HEAD_DIM:base + pair, :]], axis=0)

    ones_rows = jnp.ones((ONES_ROWS, blk), _BF16)
    key_pos = lax.broadcasted_iota(jnp.int32, (blk, blk), 0)
    qry_pos = lax.broadcasted_iota(jnp.int32, (blk, blk), 1)
    causal = key_pos <= qry_pos

    def blocks_step(js, own):
        k_pairs = []
        for j in js:
            row0 = pl.multiple_of(j * blk, blk)
            k_pairs.append([k_ref[pl.ds(row0, blk), p * pair:(p + 1) * pair] for p in range(n_heads // 2)])
        col_max = {}

        def issue_scores(h):
            for t in range(len(js)):
                s = _dot(k_pairs[t][h // 2], qm_ref[h])
                if own:
                    s = jnp.where(causal, s, MASK_NEG)
                s_ref[t, h] = s
                col_max[t, h] = jnp.max(s, axis=0, keepdims=True)

        ahead = min(QK_LOOKAHEAD[len(js)], n_heads)
        for h in range(ahead):
            issue_scores(h)
        for h in range(n_heads):
            if h + ahead < n_heads:
                issue_scores(h + ahead)
            bs = [None if own else bias_ref[h, j] for j in js]
            m_old = None if own else m_ref[h]
            m_new = m_old
            for t, b in enumerate(bs):
                cand = col_max[t, h] if b is None else col_max[t, h] + b
                m_new = cand if m_new is None else jnp.maximum(m_new, cand)
            pv = None
            for t, j in enumerate(js):
                ref_point = m_new if bs[t] is None else m_new - bs[t]
                p = jnp.exp2(s_ref[t, h] - ref_point).astype(_BF16)
                vt = jnp.concatenate([vt_ref[0, j, h * HEAD_DIM:(h + 1) * HEAD_DIM, :], ones_rows], axis=0)
                c = _dot(vt, p)
                pv = c if pv is None else pv + c
            if own:
                acc_ref[h] = pv
            else:
                acc_ref[h] = jnp.exp2(m_old - m_new) * acc_ref[h] + pv
            m_ref[h] = m_new

    def gate_and_lead(n_lead_blocks):
        blk_id = lax.broadcasted_iota(jnp.int32, (n_blocks, blk), 0)
        valid = blk_id < i
        gates = []
        for h in range(n_heads):
            base = (h // 2) * pair
            km = km_ref[0, :, base:base + pair]
            km_hi = km.astype(_BF16)
            km_lo = (km - km_hi.astype(_F32)).astype(_BF16)
            qm = qm_ref[h]
            gates.append(_dot(km_hi, qm) + _dot(km_lo, qm))

        blocks_step([i], True)

        for h in range(n_heads):
            g = jnp.where(valid, gates[h], -jnp.inf)
            picked = jnp.zeros((n_blocks, blk), _F32)
            for _ in range(MOBA_TOPK):
                g_max = jnp.max(g, axis=0, keepdims=True)
                first = jnp.min(jnp.where(g == g_max, blk_id, n_blocks), axis=0, keepdims=True)
                hit = blk_id == first
                picked = jnp.where(hit, 1.0, picked)
                g = jnp.where(hit, -jnp.inf, g)
            bias = jnp.where(valid, jnp.where(picked > 0.0, 0.0, MASK_NEG), MASK_NEG)
            for j in range(n_blocks):
                bias_ref[h, j] = bias[j:j + 1, :]

        start = 0
        while start < n_lead_blocks:
            size = (n_lead_blocks - start - 1) % BLOCKS_PER_ITER + 1
            blocks_step(list(range(start, start + size)), False)
            start += size

    n_lead = jnp.where(i >= BLOCKS_PER_ITER, i % BLOCKS_PER_ITER + BLOCKS_PER_ITER, i)
    for r in range(2 * BLOCKS_PER_ITER):
        @pl.when(n_lead == r)
        def _(r=r):
            gate_and_lead(r)

    def body(u, carry):
        blocks_step([n_lead + BLOCKS_PER_ITER * u + t for t in range(BLOCKS_PER_ITER)], False)
        return carry

    lax.fori_loop(0, (i - n_lead) // BLOCKS_PER_ITER, body, 0)

    outs = []
    for h in range(n_heads):
        acc = acc_ref[h]
        outs.append(acc[0:HEAD_DIM] * (1.0 / acc[HEAD_DIM:HEAD_DIM + 1]))
    ot = jnp.concatenate(outs, axis=0)
    o_ref[0, 0] = (ot * ga_ref[0, 0].astype(_F32)).astype(_BF16)


def _out_kernel(yp_ref, yat_ref, x_ref, w_ref, gain_ref, bias_ref, o_ref, *, d_pool, alpha):
    rows = MOBA_BLOCK
    for c in range(x_ref.shape[0] // rows):
        r = slice(c * rows, (c + 1) * rows)
        out = _dot(yp_ref[r, :], w_ref[0:d_pool, :]) + _dot_tn(yat_ref[0, c], w_ref[d_pool:, :])
        o_ref[r, :] = alpha * x_ref[r, :] + out
        for g in range(c * rows, (c + 1) * rows, LN_ROWS):
            z = o_ref[g:g + LN_ROWS, :]
            mu = jnp.mean(z, axis=-1, keepdims=True)
            zc = z - mu
            var = jnp.mean(zc * zc, axis=-1, keepdims=True)
            o_ref[g:g + LN_ROWS, :] = zc * lax.rsqrt(var + LN_EPS) * gain_ref[...] + bias_ref[...]


def _layer(h, positions, w_in, pool_w, pool_scale, w_out, ln_gain, ln_bias, alpha):
    b, s, d_model = h.shape
    d_pool = pool_scale.shape[0]
    d_attn = d_model - d_pool
    n_heads = d_attn // HEAD_DIM
    n_blocks = s // MOBA_BLOCK
    tm = ROW_TILE
    n_tiles = s // tm
    rows = b * s
    blocks_per_tile = tm // MOBA_BLOCK
    assert s % tm == 0 and tm % MOBA_BLOCK == 0 and d_attn % LANES == 0
    assert w_in.shape == (d_model, 2 * d_pool + 4 * d_attn)

    w_bf = w_in.astype(_BF16)
    freqs = ROPE_THETA ** (-jnp.arange(ROT_HALF, dtype=_F32) * 2.0 / ROT_DIM)
    freq_tab = jnp.broadcast_to(freqs[:, None], (ROT_HALF, tm))
    x2 = h.reshape(rows, d_model)
    pos3 = positions.reshape(b, 1, s)

    proj = pl.pallas_call(
        functools.partial(_proj_kernel, d_pool=d_pool, d_attn=d_attn, n_heads=n_heads, tm=tm),
        grid=(b, n_tiles),
        in_specs=[
            pl.BlockSpec((tm, d_model), lambda bi, si: (bi * n_tiles + si, 0)),
            pl.BlockSpec((1, 1, tm), lambda bi, si: (bi, 0, si)),
            pl.BlockSpec((ROT_HALF, tm), lambda bi, si: (0, 0)),
            pl.BlockSpec(w_bf.shape, lambda bi, si: (0, 0)),
            pl.BlockSpec(pool_w.shape, lambda bi, si: (0, 0, 0)),
            pl.BlockSpec((1, d_pool), lambda bi, si: (0, 0)),
        ],
        out_specs=[
            pl.BlockSpec((tm, d_pool), lambda bi, si: (bi * n_tiles + si, 0)),
            pl.BlockSpec((1, blocks_per_tile, d_attn, MOBA_BLOCK), lambda bi, si: (bi, si, 0, 0)),
            pl.BlockSpec((tm, d_attn), lambda bi, si: (bi * n_tiles + si, 0)),
            pl.BlockSpec((1, blocks_per_tile, d_attn, MOBA_BLOCK), lambda bi, si: (bi, si, 0, 0)),
            pl.BlockSpec((blocks_per_tile, 1, d_attn), lambda bi, si: (bi * n_tiles + si, 0, 0)),
            pl.BlockSpec((1, blocks_per_tile, d_attn, MOBA_BLOCK), lambda bi, si: (bi, si, 0, 0)),
        ],
        out_shape=[
            jax.ShapeDtypeStruct((rows, d_pool), _BF16),
            jax.ShapeDtypeStruct((b, n_blocks, d_attn, MOBA_BLOCK), _BF16),
            jax.ShapeDtypeStruct((rows, d_attn), _BF16),
            jax.ShapeDtypeStruct((b, n_blocks, d_attn, MOBA_BLOCK), _BF16),
            jax.ShapeDtypeStruct((b * n_blocks, 1, d_attn), _F32),
            jax.ShapeDtypeStruct((b, n_blocks, d_attn, MOBA_BLOCK), _BF16),
        ],
        scratch_shapes=[pltpu.VMEM((tm + POOL_HALO, d_pool), _F32)],
        compiler_params=pltpu.CompilerParams(
            dimension_semantics=("arbitrary", "arbitrary"), vmem_limit_bytes=VMEM_LIMIT),
        name="proj_pool_rope",
    )
    y_pool, q_t, k_rot, v_t, k_mean, g_attn = proj(
        x2, pos3, freq_tab, w_bf, pool_w.astype(_BF16), pool_scale.reshape(1, d_pool))

    k_mean = k_mean.reshape(b, n_blocks, d_attn)
    attn = pl.pallas_call(
        functools.partial(_attn_kernel, n_blocks=n_blocks, n_heads=n_heads),
        grid=(b, n_blocks),
        in_specs=[
            pl.BlockSpec((1, 1, d_attn, MOBA_BLOCK), lambda bi, i: (bi, i, 0, 0)),
            pl.BlockSpec((s, d_attn), lambda bi, i: (bi, 0)),
            pl.BlockSpec((1, n_blocks, d_attn, MOBA_BLOCK), lambda bi, i: (bi, 0, 0, 0)),
            pl.BlockSpec((1, n_blocks, d_attn), lambda bi, i: (bi, 0, 0)),
            pl.BlockSpec((1, 1, d_attn, MOBA_BLOCK), lambda bi, i: (bi, i, 0, 0)),
        ],
        out_specs=pl.BlockSpec((1, 1, d_attn, MOBA_BLOCK), lambda bi, i: (bi, i, 0, 0)),
        out_shape=jax.ShapeDtypeStruct((b, n_blocks, d_attn, MOBA_BLOCK), _BF16),
        scratch_shapes=[
            pltpu.VMEM((n_heads, 2 * HEAD_DIM, MOBA_BLOCK), _BF16),
            pltpu.VMEM((n_heads, n_blocks, 1, MOBA_BLOCK), _F32),
            pltpu.VMEM((n_heads, 1, MOBA_BLOCK), _F32),
            pltpu.VMEM((n_heads, HEAD_DIM + ONES_ROWS, MOBA_BLOCK), _F32),
            pltpu.VMEM((BLOCKS_PER_ITER, n_heads, MOBA_BLOCK, MOBA_BLOCK), _F32),
        ],
        compiler_params=pltpu.CompilerParams(
            dimension_semantics=("arbitrary", "arbitrary"), vmem_limit_bytes=VMEM_LIMIT),
        name="moba_attention",
    )
    y_attn = attn(q_t, k_rot, v_t, k_mean, g_attn)

    out_blocks = OUT_ROW_TILE // MOBA_BLOCK
    out_tiles_per_seq = s // OUT_ROW_TILE
    assert s % OUT_ROW_TILE == 0 and OUT_ROW_TILE % MOBA_BLOCK == 0
    outp = pl.pallas_call(
        functools.partial(_out_kernel, d_pool=d_pool, alpha=alpha),
        grid=(rows // OUT_ROW_TILE,),
        in_specs=[
            pl.BlockSpec((OUT_ROW_TILE, d_pool), lambda r: (r, 0)),
            pl.BlockSpec((1, out_blocks, d_attn, MOBA_BLOCK),
                         lambda r: (r // out_tiles_per_seq, r % out_tiles_per_seq, 0, 0)),
            pl.BlockSpec((OUT_ROW_TILE, d_model), lambda r: (r, 0)),
            pl.BlockSpec((d_model, d_model), lambda r: (0, 0)),
            pl.BlockSpec((1, d_model), lambda r: (0, 0)),
            pl.BlockSpec((1, d_model), lambda r: (0, 0)),
        ],
        out_specs=pl.BlockSpec((OUT_ROW_TILE, d_model), lambda r: (r, 0)),
        out_shape=jax.ShapeDtypeStruct((rows, d_model), h.dtype),
        compiler_params=pltpu.CompilerParams(
            dimension_semantics=("arbitrary",), vmem_limit_bytes=VMEM_LIMIT),
        name="out_proj_layernorm",
    )
    out = outp(y_pool, y_attn, x2, w_out.astype(_BF16),
               ln_gain.reshape(1, d_model), ln_bias.reshape(1, d_model))
    return out.reshape(b, s, d_model)


def kernel(x, positions, w_in, pool_w, pool_scale, w_out, ln_gain, ln_bias):
    depth = w_in.shape[0]
    alpha = (2.0 * depth) ** 0.25
    h = x
    for layer in range(depth):
        h = _layer(h, positions, w_in[layer], pool_w[layer], pool_scale[layer],
                   w_out[layer], ln_gain[layer], ln_bias[layer], alpha)
    return h
```

```python
import functools

import jax
import jax.numpy as jnp
from jax import lax
from jax.experimental import pallas as pl
from jax.experimental.pallas import tpu as pltpu

POOL_WINDOWS = (2, 4, 8, 16)
HEAD_DIM = 64
ROT_DIM = HEAD_DIM // 4
ROT_HALF = ROT_DIM // 2
ROPE_THETA = 500000.0
MOBA_BLOCK = 256
MOBA_TOPK = 3
LN_EPS = 1e-5

LANES = 128
POOL_HALO = 16
ROW_TILE = 1024
MASK_NEG = -1e30
ONES_ROWS = 16
LOG2_E = 1.4426950408889634
BLOCKS_PER_ITER = 4
QK_LOOKAHEAD = {1: 8, 2: 3, 3: 3, 4: 2}
OUT_ROW_TILE = 1024
VMEM_LIMIT = 48 * 1024 * 1024

_F32 = jnp.float32
_BF16 = jnp.bfloat16


def _silu(x):
    half = 0.5 * x
    return half + half * jnp.tanh(half)


def _dot(a, b):
    return jnp.dot(a, b, preferred_element_type=_F32)


def _dot_tt(a, b):
    return lax.dot_general(a, b, (((0,), (1,)), ((), ())), preferred_element_type=_F32)


def _dot_tn(a, b):
    return lax.dot_general(a, b, (((0,), (0,)), ((), ())), preferred_element_type=_F32)


def _rotate_heads(t, cos, sin, n_heads):
    pieces = []
    for h in range(n_heads):
        base = h * HEAD_DIM
        x1 = t[base:base + ROT_HALF]
        x2 = t[base + ROT_HALF:base + ROT_DIM]
        pieces.append(x1 * cos - x2 * sin)
        pieces.append(x2 * cos + x1 * sin)
        pieces.append(t[base + ROT_DIM:base + HEAD_DIM])
    return jnp.concatenate(pieces, axis=0)


def _proj_kernel(x_ref, pos_ref, freq_ref, w_ref, poolw_ref, pscale_ref,
                 ypool_ref, qt_ref, k_ref, vt_ref, km_ref, gat_ref, ubuf_ref,
                 *, d_pool, d_attn, n_heads, tm):
    s_idx = pl.program_id(1)
    c_q = 2 * d_pool

    @pl.when((pl.program_id(0) == 0) & (s_idx == 0))
    def _():
        ubuf_ref[0:POOL_HALO, :] = jnp.zeros((POOL_HALO, d_pool), _F32)

    xb = x_ref[...].astype(_BF16)

    kt_raw = _dot_tt(w_ref[:, c_q + d_attn:c_q + 2 * d_attn], xb)
    std = _dot(xb, w_ref[:, 0:c_q])
    qt_raw = _dot_tt(w_ref[:, c_q:c_q + d_attn], xb)
    gat_raw = _dot_tt(w_ref[:, c_q + 3 * d_attn:c_q + 4 * d_attn], xb)
    vt_raw = _dot_tt(w_ref[:, c_q + 2 * d_attn:c_q + 3 * d_attn], xb)

    ubuf_ref[POOL_HALO:POOL_HALO + tm, :] = std[:, 0:d_pool]
    group = d_pool // len(POOL_WINDOWS)
    t_head = s_idx * tm + lax.broadcasted_iota(jnp.int32, (POOL_HALO, group), 0)
    for g, w in enumerate(POOL_WINDOWS):
        cols = slice(g * group, (g + 1) * group)
        u_g = ubuf_ref[POOL_HALO:POOL_HALO + tm, cols]
        tsum = u_g
        for s in range(1, w):
            tsum = tsum + ubuf_ref[POOL_HALO - s:POOL_HALO - s + tm, cols]
        count_head = jnp.minimum(t_head + 1, w).astype(_F32)
        mean_head = tsum[0:POOL_HALO] / count_head
        mean = jnp.concatenate([mean_head, tsum[POOL_HALO:] * (1.0 / w)], axis=0)
        d = mean - u_g
        y = _dot(d.astype(_BF16), poolw_ref[g])
        gate = _silu(std[:, d_pool + g * group:d_pool + (g + 1) * group])
        ypool_ref[:, cols] = (y * pscale_ref[:, cols] * gate).astype(_BF16)
    last_tile = s_idx == pl.num_programs(1) - 1
    ubuf_ref[0:POOL_HALO, :] = jnp.where(last_tile, 0.0, ubuf_ref[tm:tm + POOL_HALO, :])

    gat = _silu(gat_raw).astype(_BF16)
    for r in range(tm // MOBA_BLOCK):
        gat_ref[0, r] = gat[:, r * MOBA_BLOCK:(r + 1) * MOBA_BLOCK]

    ang = freq_ref[...] * pos_ref[0].astype(_F32)
    cos = jnp.cos(ang)
    sin = jnp.sin(ang)

    scale = HEAD_DIM ** -0.5 * LOG2_E
    qt = _rotate_heads(qt_raw, cos, sin, n_heads) * scale
    qt = qt.astype(_BF16)
    for r in range(tm // MOBA_BLOCK):
        qt_ref[0, r] = qt[:, r * MOBA_BLOCK:(r + 1) * MOBA_BLOCK]

    kt = _rotate_heads(kt_raw, cos, sin, n_heads)
    for c in range(d_attn // LANES):
        kc = kt[c * LANES:(c + 1) * LANES].T
        k_ref[:, c * LANES:(c + 1) * LANES] = kc.astype(_BF16)
        for r in range(tm // MOBA_BLOCK):
            blk_sum = jnp.sum(kc[r * MOBA_BLOCK:(r + 1) * MOBA_BLOCK], axis=0, keepdims=True)
            km_ref[r, :, c * LANES:(c + 1) * LANES] = blk_sum * (1.0 / MOBA_BLOCK)

    vt = vt_raw.astype(_BF16)
    for r in range(tm // MOBA_BLOCK):
        vt_ref[0, r] = vt[:, r * MOBA_BLOCK:(r + 1) * MOBA_BLOCK]


def _attn_kernel(qt_ref, k_ref, vt_ref, km_ref, ga_ref, o_ref,
                 qm_ref, bias_ref, m_ref, acc_ref, s_ref, *, n_blocks, n_heads):
    i = pl.program_id(1)
    blk = MOBA_BLOCK
    pair = 2 * HEAD_DIM

    zeros_half = jnp.zeros((HEAD_DIM, blk), _BF16)
    for h in range(n_heads):
        base = (h // 2) * pair
        if h % 2 == 0:
            qm_ref[h] = jnp.concatenate([qt_ref[0, 0, base:base + HEAD_DIM, :], zeros_half], axis=0)
        else:
            qm_ref[h] = jnp.concatenate([zeros_half, qt_ref[0, 0, base + HEAD_DIM:base + pair, :]], axis=0)

    ones_rows = jnp.ones((ONES_ROWS, blk), _BF16)
    key_pos = lax.broadcasted_iota(jnp.int32, (blk, blk), 0)
    qry_pos = lax.broadcasted_iota(jnp.int32, (blk, blk), 1)
    causal = key_pos <= qry_pos

    def blocks_step(js, own):
        k_pairs = []
        for j in js:
            row0 = pl.multiple_of(j * blk, blk)
            k_pairs.append([k_ref[pl.ds(row0, blk), p * pair:(p + 1) * pair] for p in range(n_heads // 2)])
        col_max = {}

        def issue_scores(h):
            for t in range(len(js)):
                s = _dot(k_pairs[t][h // 2], qm_ref[h])
                if own:
                    s = jnp.where(causal, s, MASK_NEG)
                s_ref[t, h] = s
                col_max[t, h] = jnp.max(s, axis=0, keepdims=True)

        ahead = min(QK_LOOKAHEAD[len(js)], n_heads)
        for h in range(ahead):
            issue_scores(h)
        for h in range(n_heads):
            if h + ahead < n_heads:
                issue_scores(h + ahead)
            bs = [None if own else bias_ref[h, pl.ds(j, 1), :] for j in js]
            m_old = None if own else m_ref[h]
            m_new = m_old
            for t, b in enumerate(bs):
                cand = col_max[t, h] if b is None else col_max[t, h] + b
                m_new = cand if m_new is None else jnp.maximum(m_new, cand)
            pv = None
            for t, j in enumerate(js):
                ref_point = m_new if bs[t] is None else m_new - bs[t]
                p = jnp.exp2(s_ref[t, h] - ref_point).astype(_BF16)
                vt = jnp.concatenate([vt_ref[0, j, h * HEAD_DIM:(h + 1) * HEAD_DIM, :], ones_rows], axis=0)
                c = _dot(vt, p)
                pv = c if pv is None else pv + c
            if own:
                acc_ref[h] = pv
            else:
                acc_ref[h] = jnp.exp2(m_old - m_new) * acc_ref[h] + pv
            m_ref[h] = m_new

    def gate_and_lead(n_lead_blocks):
        blk_id = lax.broadcasted_iota(jnp.int32, (n_blocks, blk), 0)
        valid = blk_id < i
        gates = []
        for h in range(n_heads):
            base = (h // 2) * pair
            km = km_ref[0, :, base:base + pair]
            km_hi = km.astype(_BF16)
            km_lo = (km - km_hi.astype(_F32)).astype(_BF16)
            qm = qm_ref[h]
            gates.append(_dot(km_hi, qm) + _dot(km_lo, qm))

        blocks_step([i], True)

        for h in range(n_heads):
            g = jnp.where(valid, gates[h], -jnp.inf)
            picked = jnp.zeros((n_blocks, blk), _F32)
            for _ in range(MOBA_TOPK):
                g_max = jnp.max(g, axis=0, keepdims=True)
                first = jnp.min(jnp.where(g == g_max, blk_id, n_blocks), axis=0, keepdims=True)
                hit = blk_id == first
                picked = jnp.where(hit, 1.0, picked)
                g = jnp.where(hit, -jnp.inf, g)
            bias = jnp.where(valid, jnp.where(picked > 0.0, 0.0, MASK_NEG), MASK_NEG)
            bias_ref[h] = bias

        start = 0
        while start < n_lead_blocks:
            size = (n_lead_blocks - start - 1) % BLOCKS_PER_ITER + 1
            blocks_step(list(range(start, start + size)), False)
            start += size

    n_lead = jnp.where(i >= BLOCKS_PER_ITER, i % BLOCKS_PER_ITER + BLOCKS_PER_ITER, i)
    for r in range(2 * BLOCKS_PER_ITER):
        @pl.when(n_lead == r)
        def _(r=r):
            gate_and_lead(r)

    def body(u, carry):
        blocks_step([n_lead + BLOCKS_PER_ITER * u + t for t in range(BLOCKS_PER_ITER)], False)
        return carry

    lax.fori_loop(0, (i - n_lead) // BLOCKS_PER_ITER, body, 0)

    outs = []
    for h in range(n_heads):
        acc = acc_ref[h]
        outs.append(acc[0:HEAD_DIM] * (1.0 / acc[HEAD_DIM:HEAD_DIM + 1]))
    ot = jnp.concatenate(outs, axis=0)
    o_ref[0, 0] = (ot * ga_ref[0, 0].astype(_F32)).astype(_BF16)


def _out_kernel(yp_ref, yat_ref, x_ref, w_ref, gain_ref, bias_ref, o_ref, *, d_pool, alpha):
    rows = MOBA_BLOCK
    for c in range(x_ref.shape[0] // rows):
        r = slice(c * rows, (c + 1) * rows)
        out = _dot(yp_ref[r, :], w_ref[0:d_pool, :]) + _dot_tn(yat_ref[0, c], w_ref[d_pool:, :])
        z = alpha * x_ref[r, :] + out
        mu = jnp.mean(z, axis=-1, keepdims=True)
        zc = z - mu
        var = jnp.mean(zc * zc, axis=-1, keepdims=True)
        o_ref[r, :] = zc * lax.rsqrt(var + LN_EPS) * gain_ref[...] + bias_ref[...]


def _layer(h, positions, w_in, pool_w, pool_scale, w_out, ln_gain, ln_bias, alpha):
    b, s, d_model = h.shape
    d_pool = pool_scale.shape[0]
    d_attn = d_model - d_pool
    n_heads = d_attn // HEAD_DIM
    n_blocks = s // MOBA_BLOCK
    tm = ROW_TILE
    n_tiles = s // tm
    rows = b * s
    blocks_per_tile = tm // MOBA_BLOCK
    assert s % tm == 0 and tm % MOBA_BLOCK == 0 and d_attn % LANES == 0
    assert w_in.shape == (d_model, 2 * d_pool + 4 * d_attn)

    w_bf = w_in.astype(_BF16)
    freqs = ROPE_THETA ** (-jnp.arange(ROT_HALF, dtype=_F32) * 2.0 / ROT_DIM)
    freq_tab = jnp.broadcast_to(freqs[:, None], (ROT_HALF, tm))
    x2 = h.reshape(rows, d_model)
    pos3 = positions.reshape(b, 1, s)

    proj = pl.pallas_call(
        functools.partial(_proj_kernel, d_pool=d_pool, d_attn=d_attn, n_heads=n_heads, tm=tm),
        grid=(b, n_tiles),
        in_specs=[
            pl.BlockSpec((tm, d_model), lambda bi, si: (bi * n_tiles + si, 0)),
            pl.BlockSpec((1, 1, tm), lambda bi, si: (bi, 0, si)),
            pl.BlockSpec((ROT_HALF, tm), lambda bi, si: (0, 0)),
            pl.BlockSpec(w_bf.shape, lambda bi, si: (0, 0)),
            pl.BlockSpec(pool_w.shape, lambda bi, si: (0, 0, 0)),
            pl.BlockSpec((1, d_pool), lambda bi, si: (0, 0)),
        ],
        out_specs=[
            pl.BlockSpec((tm, d_pool), lambda bi, si: (bi * n_tiles + si, 0)),
            pl.BlockSpec((1, blocks_per_tile, d_attn, MOBA_BLOCK), lambda bi, si: (bi, si, 0, 0)),
            pl.BlockSpec((tm, d_attn), lambda bi, si: (bi * n_tiles + si, 0)),
            pl.BlockSpec((1, blocks_per_tile, d_attn, MOBA_BLOCK), lambda bi, si: (bi, si, 0, 0)),
            pl.BlockSpec((blocks_per_tile, 1, d_attn), lambda bi, si: (bi * n_tiles + si, 0, 0)),
            pl.BlockSpec((1, blocks_per_tile, d_attn, MOBA_BLOCK), lambda bi, si: (bi, si, 0, 0)),
        ],
        out_shape=[
            jax.ShapeDtypeStruct((rows, d_pool), _BF16),
            jax.ShapeDtypeStruct((b, n_blocks, d_attn, MOBA_BLOCK), _BF16),
            jax.ShapeDtypeStruct((rows, d_attn), _BF16),
            jax.ShapeDtypeStruct((b, n_blocks, d_attn, MOBA_BLOCK), _BF16),
            jax.ShapeDtypeStruct((b * n_blocks, 1, d_attn), _F32),
            jax.ShapeDtypeStruct((b, n_blocks, d_attn, MOBA_BLOCK), _BF16),
        ],
        scratch_shapes=[pltpu.VMEM((tm + POOL_HALO, d_pool), _F32)],
        compiler_params=pltpu.CompilerParams(
            dimension_semantics=("arbitrary", "arbitrary"), vmem_limit_bytes=VMEM_LIMIT),
        name="proj_pool_rope",
    )
    y_pool, q_t, k_rot, v_t, k_mean, g_attn = proj(
        x2, pos3, freq_tab, w_bf, pool_w.astype(_BF16), pool_scale.reshape(1, d_pool))

    k_mean = k_mean.reshape(b, n_blocks, d_attn)
    attn = pl.pallas_call(
        functools.partial(_attn_kernel, n_blocks=n_blocks, n_heads=n_heads),
        grid=(b, n_blocks),
        in_specs=[
            pl.BlockSpec((1, 1, d_attn, MOBA_BLOCK), lambda bi, i: (bi, i, 0, 0)),
            pl.BlockSpec((s, d_attn), lambda bi, i: (bi, 0)),
            pl.BlockSpec((1, n_blocks, d_attn, MOBA_BLOCK), lambda bi, i: (bi, 0, 0, 0)),
            pl.BlockSpec((1, n_blocks, d_attn), lambda bi, i: (bi, 0, 0)),
            pl.BlockSpec((1, 1, d_attn, MOBA_BLOCK), lambda bi, i: (bi, i, 0, 0)),
        ],
        out_specs=pl.BlockSpec((1, 1, d_attn, MOBA_BLOCK), lambda bi, i: (bi, i, 0, 0)),
        out_shape=jax.ShapeDtypeStruct((b, n_blocks, d_attn, MOBA_BLOCK), _BF16),
        scratch_shapes=[
            pltpu.VMEM((n_heads, 2 * HEAD_DIM, MOBA_BLOCK), _BF16),
            pltpu.VMEM((n_heads, n_blocks, MOBA_BLOCK), _F32),
            pltpu.VMEM((n_heads, 1, MOBA_BLOCK), _F32),
            pltpu.VMEM((n_heads, HEAD_DIM + ONES_ROWS, MOBA_BLOCK), _F32),
            pltpu.VMEM((BLOCKS_PER_ITER, n_heads, MOBA_BLOCK, MOBA_BLOCK), _F32),
        ],
        compiler_params=pltpu.CompilerParams(
            dimension_semantics=("arbitrary", "arbitrary"), vmem_limit_bytes=VMEM_LIMIT),
        name="moba_attention",
    )
    y_attn = attn(q_t, k_rot, v_t, k_mean, g_attn)

    out_blocks = OUT_ROW_TILE // MOBA_BLOCK
    out_tiles_per_seq = s // OUT_ROW_TILE
    assert s % OUT_ROW_TILE == 0 and OUT_ROW_TILE % MOBA_BLOCK == 0
    outp = pl.pallas_call(
        functools.partial(_out_kernel, d_pool=d_pool, alpha=alpha),
        grid=(rows // OUT_ROW_TILE,),
        in_specs=[
            pl.BlockSpec((OUT_ROW_TILE, d_pool), lambda r: (r, 0)),
            pl.BlockSpec((1, out_blocks, d_attn, MOBA_BLOCK),
                         lambda r: (r // out_tiles_per_seq, r % out_tiles_per_seq, 0, 0)),
            pl.BlockSpec((OUT_ROW_TILE, d_model), lambda r: (r, 0)),
            pl.BlockSpec((d_model, d_model), lambda r: (0, 0)),
            pl.BlockSpec((1, d_model), lambda r: (0, 0)),
            pl.BlockSpec((1, d_model), lambda r: (0, 0)),
        ],
        out_specs=pl.BlockSpec((OUT_ROW_TILE, d_model), lambda r: (r, 0)),
        out_shape=jax.ShapeDtypeStruct((rows, d_model), h.dtype),
        compiler_params=pltpu.CompilerParams(
            dimension_semantics=("arbitrary",), vmem_limit_bytes=VMEM_LIMIT),
        name="out_proj_layernorm",
    )
    out = outp(y_pool, y_attn, x2, w_out.astype(_BF16),
               ln_gain.reshape(1, d_model), ln_bias.reshape(1, d_model))
    return out.reshape(b, s, d_model)


def kernel(x, positions, w_in, pool_w, pool_scale, w_out, ln_gain, ln_bias):
    depth = w_in.shape[0]
    alpha = (2.0 * depth) ** 0.25
    h = x
    for layer in range(depth):
        h = _layer(h, positions, w_in[layer], pool_w[layer], pool_scale[layer],
                   w_out[layer], ln_gain[layer], ln_bias[layer], alpha)
    return h
```

```python
import functools

import jax
import jax.numpy as jnp
from jax import lax
from jax.experimental import pallas as pl
from jax.experimental.pallas import tpu as pltpu

POOL_WINDOWS = (2, 4, 8, 16)
HEAD_DIM = 64
ROT_DIM = HEAD_DIM // 4
ROT_HALF = ROT_DIM // 2
ROPE_THETA = 500000.0
MOBA_BLOCK = 256
MOBA_TOPK = 3
LN_EPS = 1e-5

LANES = 128
POOL_HALO = 16
ROW_TILE = 1024
MASK_NEG = -1e30
ONES_ROWS = 16
LOG2_E = 1.4426950408889634
BLOCKS_PER_ITER = 4
QK_LOOKAHEAD = {1: 8, 2: 3, 3: 3, 4: 2}
OUT_CHUNKS = 2
VMEM_LIMIT = 48 * 1024 * 1024

_F32 = jnp.float32
_BF16 = jnp.bfloat16


def _silu(x):
    half = 0.5 * x
    return half + half * jnp.tanh(half)


def _dot(a, b):
    return jnp.dot(a, b, preferred_element_type=_F32)


def _dot_tt(a, b):
    return lax.dot_general(a, b, (((0,), (1,)), ((), ())), preferred_element_type=_F32)


def _dot_tn(a, b):
    return lax.dot_general(a, b, (((0,), (0,)), ((), ())), preferred_element_type=_F32)


def _rotate_heads(t, cos, sin, n_heads):
    pieces = []
    for h in range(n_heads):
        base = h * HEAD_DIM
        x1 = t[base:base + ROT_HALF]
        x2 = t[base + ROT_HALF:base + ROT_DIM]
        pieces.append(x1 * cos - x2 * sin)
        pieces.append(x2 * cos + x1 * sin)
        pieces.append(t[base + ROT_DIM:base + HEAD_DIM])
    return jnp.concatenate(pieces, axis=0)


def _proj_kernel(x_ref, pos_ref, freq_ref, w_ref, poolw_ref, pscale_ref,
                 ypool_ref, qt_ref, k_ref, vt_ref, km_ref, gat_ref, ubuf_ref,
                 *, d_pool, d_attn, n_heads, tm):
    s_idx = pl.program_id(1)
    c_q = 2 * d_pool

    @pl.when((pl.program_id(0) == 0) & (s_idx == 0))
    def _():
        ubuf_ref[0:POOL_HALO, :] = jnp.zeros((POOL_HALO, d_pool), _F32)

    xb = x_ref[...].astype(_BF16)

    kt_raw = _dot_tt(w_ref[:, c_q + d_attn:c_q + 2 * d_attn], xb)
    std = _dot(xb, w_ref[:, 0:c_q])
    qt_raw = _dot_tt(w_ref[:, c_q:c_q + d_attn], xb)
    gat_raw = _dot_tt(w_ref[:, c_q + 3 * d_attn:c_q + 4 * d_attn], xb)
    vt_raw = _dot_tt(w_ref[:, c_q + 2 * d_attn:c_q + 3 * d_attn], xb)

    ubuf_ref[POOL_HALO:POOL_HALO + tm, :] = std[:, 0:d_pool]
    group = d_pool // len(POOL_WINDOWS)
    t_head = s_idx * tm + lax.broadcasted_iota(jnp.int32, (POOL_HALO, group), 0)
    for g, w in enumerate(POOL_WINDOWS):
        cols = slice(g * group, (g + 1) * group)
        u_g = ubuf_ref[POOL_HALO:POOL_HALO + tm, cols]
        tsum = u_g
        for s in range(1, w):
            tsum = tsum + ubuf_ref[POOL_HALO - s:POOL_HALO - s + tm, cols]
        count_head = jnp.minimum(t_head + 1, w).astype(_F32)
        mean_head = tsum[0:POOL_HALO] / count_head
        mean = jnp.concatenate([mean_head, tsum[POOL_HALO:] * (1.0 / w)], axis=0)
        d = mean - u_g
        y = _dot(d.astype(_BF16), poolw_ref[g])
        gate = _silu(std[:, d_pool + g * group:d_pool + (g + 1) * group])
        ypool_ref[:, cols] = (y * pscale_ref[:, cols] * gate).astype(_BF16)
    last_tile = s_idx == pl.num_programs(1) - 1
    ubuf_ref[0:POOL_HALO, :] = jnp.where(last_tile, 0.0, ubuf_ref[tm:tm + POOL_HALO, :])

    gat = _silu(gat_raw).astype(_BF16)
    for r in range(tm // MOBA_BLOCK):
        gat_ref[0, r] = gat[:, r * MOBA_BLOCK:(r + 1) * MOBA_BLOCK]

    ang = freq_ref[...] * pos_ref[0].astype(_F32)
    cos = jnp.cos(ang)
    sin = jnp.sin(ang)

    scale = HEAD_DIM ** -0.5 * LOG2_E
    qt = _rotate_heads(qt_raw, cos, sin, n_heads) * scale
    qt = qt.astype(_BF16)
    for r in range(tm // MOBA_BLOCK):
        qt_ref[0, r] = qt[:, r * MOBA_BLOCK:(r + 1) * MOBA_BLOCK]

    kt = _rotate_heads(kt_raw, cos, sin, n_heads)
    for c in range(d_attn // LANES):
        kc = kt[c * LANES:(c + 1) * LANES].T
        k_ref[:, c * LANES:(c + 1) * LANES] = kc.astype(_BF16)
        for r in range(tm // MOBA_BLOCK):
            blk_sum = jnp.sum(kc[r * MOBA_BLOCK:(r + 1) * MOBA_BLOCK], axis=0, keepdims=True)
            km_ref[r, :, c * LANES:(c + 1) * LANES] = blk_sum * (1.0 / MOBA_BLOCK)

    vt = vt_raw.astype(_BF16)
    for r in range(tm // MOBA_BLOCK):
        vt_ref[0, r] = vt[:, r * MOBA_BLOCK:(r + 1) * MOBA_BLOCK]


def _attn_kernel(qt_ref, k_ref, vt_ref, km_ref, ga_ref, yp_ref, x_ref, wout_ref, gain_ref, lnb_ref, o_ref,
                 qm_ref, bias_ref, m_ref, acc_ref, s_ref, *, n_blocks, n_heads, d_pool, alpha):
    i = pl.program_id(1)
    blk = MOBA_BLOCK
    pair = 2 * HEAD_DIM

    zeros_half = jnp.zeros((HEAD_DIM, blk), _BF16)
    for h in range(n_heads):
        base = (h // 2) * pair
        if h % 2 == 0:
            qm_ref[h] = jnp.concatenate([qt_ref[0, 0, base:base + HEAD_DIM, :], zeros_half], axis=0)
        else:
            qm_ref[h] = jnp.concatenate([zeros_half, qt_ref[0, 0, base + HEAD_DIM:base + pair, :]], axis=0)

    ones_rows = jnp.ones((ONES_ROWS, blk), _BF16)
    key_pos = lax.broadcasted_iota(jnp.int32, (blk, blk), 0)
    qry_pos = lax.broadcasted_iota(jnp.int32, (blk, blk), 1)
    causal = key_pos <= qry_pos

    def blocks_step(js, own):
        k_pairs = []
        for j in js:
            row0 = pl.multiple_of(j * blk, blk)
            k_pairs.append([k_ref[pl.ds(row0, blk), p * pair:(p + 1) * pair] for p in range(n_heads // 2)])
        col_max = {}

        def issue_scores(h):
            for t in range(len(js)):
                s = _dot(k_pairs[t][h // 2], qm_ref[h])
                if own:
                    s = jnp.where(causal, s, MASK_NEG)
                s_ref[t, h] = s
                col_max[t, h] = jnp.max(s, axis=0, keepdims=True)

        ahead = min(QK_LOOKAHEAD[len(js)], n_heads)
        for h in range(ahead):
            issue_scores(h)
        for h in range(n_heads):
            if h + ahead < n_heads:
                issue_scores(h + ahead)
            bs = [None if own else bias_ref[h, pl.ds(j, 1), :] for j in js]
            m_old = None if own else m_ref[h]
            m_new = m_old
            for t, b in enumerate(bs):
                cand = col_max[t, h] if b is None else col_max[t, h] + b
                m_new = cand if m_new is None else jnp.maximum(m_new, cand)
            pv = None
            for t, j in enumerate(js):
                ref_point = m_new if bs[t] is None else m_new - bs[t]
                p = jnp.exp2(s_ref[t, h] - ref_point).astype(_BF16)
                vt = jnp.concatenate([vt_ref[0, j, h * HEAD_DIM:(h + 1) * HEAD_DIM, :], ones_rows], axis=0)
                c = _dot(vt, p)
                pv = c if pv is None else pv + c
            if own:
                acc_ref[h] = pv
            else:
                acc_ref[h] = jnp.exp2(m_old - m_new) * acc_ref[h] + pv
            m_ref[h] = m_new

    def gate_and_lead(n_lead_blocks):
        blk_id = lax.broadcasted_iota(jnp.int32, (n_blocks, blk), 0)
        valid = blk_id < i
        gates = []
        for h in range(n_heads):
            base = (h // 2) * pair
            km = km_ref[0, :, base:base + pair]
            km_hi = km.astype(_BF16)
            km_lo = (km - km_hi.astype(_F32)).astype(_BF16)
            qm = qm_ref[h]
            gates.append(_dot(km_hi, qm) + _dot(km_lo, qm))

        blocks_step([i], True)

        for h in range(n_heads):
            g = jnp.where(valid, gates[h], -jnp.inf)
            picked = jnp.zeros((n_blocks, blk), _F32)
            for _ in range(MOBA_TOPK):
                g_max = jnp.max(g, axis=0, keepdims=True)
                first = jnp.min(jnp.where(g == g_max, blk_id, n_blocks), axis=0, keepdims=True)
                hit = blk_id == first
                picked = jnp.where(hit, 1.0, picked)
                g = jnp.where(hit, -jnp.inf, g)
            bias = jnp.where(valid, jnp.where(picked > 0.0, 0.0, MASK_NEG), MASK_NEG)
            bias_ref[h] = bias

        start = 0
        while start < n_lead_blocks:
            size = (n_lead_blocks - start - 1) % BLOCKS_PER_ITER + 1
            blocks_step(list(range(start, start + size)), False)
            start += size

    n_lead = jnp.where(i >= BLOCKS_PER_ITER, i % BLOCKS_PER_ITER + BLOCKS_PER_ITER, i)
    for r in range(2 * BLOCKS_PER_ITER):
        @pl.when(n_lead == r)
        def _(r=r):
            gate_and_lead(r)

    def body(u, carry):
        blocks_step([n_lead + BLOCKS_PER_ITER * u + t for t in range(BLOCKS_PER_ITER)], False)
        return carry

    lax.fori_loop(0, (i - n_lead) // BLOCKS_PER_ITER, body, 0)

    outs = []
    for h in range(n_heads):
        acc = acc_ref[h]
        outs.append(acc[0:HEAD_DIM] * (1.0 / acc[HEAD_DIM:HEAD_DIM + 1]))
    ot = jnp.concatenate(outs, axis=0)
    y_attn_t = (ot * ga_ref[0, 0].astype(_F32)).astype(_BF16)

    rows = blk // OUT_CHUNKS
    for c in range(OUT_CHUNKS):
        r = slice(c * rows, (c + 1) * rows)
        out = _dot(yp_ref[r, :], wout_ref[0:d_pool, :]) + _dot_tn(y_attn_t[:, r], wout_ref[d_pool:, :])
        z = alpha * x_ref[r, :] + out
        mu = jnp.mean(z, axis=-1, keepdims=True)
        zc = z - mu
        var = jnp.mean(zc * zc, axis=-1, keepdims=True)
        o_ref[r, :] = zc * lax.rsqrt(var + LN_EPS) * gain_ref[...] + lnb_ref[...]


def _layer(h, positions, w_in, pool_w, pool_scale, w_out, ln_gain, ln_bias, alpha):
    b, s, d_model = h.shape
    d_pool = pool_scale.shape[0]
    d_attn = d_model - d_pool
    n_heads = d_attn // HEAD_DIM
    n_blocks = s // MOBA_BLOCK
    tm = ROW_TILE
    n_tiles = s // tm
    rows = b * s
    blocks_per_tile = tm // MOBA_BLOCK
    assert s % tm == 0 and tm % MOBA_BLOCK == 0 and d_attn % LANES == 0
    assert w_in.shape == (d_model, 2 * d_pool + 4 * d_attn)

    w_bf = w_in.astype(_BF16)
    freqs = ROPE_THETA ** (-jnp.arange(ROT_HALF, dtype=_F32) * 2.0 / ROT_DIM)
    freq_tab = jnp.broadcast_to(freqs[:, None], (ROT_HALF, tm))
    x2 = h.reshape(rows, d_model)
    pos3 = positions.reshape(b, 1, s)

    proj = pl.pallas_call(
        functools.partial(_proj_kernel, d_pool=d_pool, d_attn=d_attn, n_heads=n_heads, tm=tm),
        grid=(b, n_tiles),
        in_specs=[
            pl.BlockSpec((tm, d_model), lambda bi, si: (bi * n_tiles + si, 0)),
            pl.BlockSpec((1, 1, tm), lambda bi, si: (bi, 0, si)),
            pl.BlockSpec((ROT_HALF, tm), lambda bi, si: (0, 0)),
            pl.BlockSpec(w_bf.shape, lambda bi, si: (0, 0)),
            pl.BlockSpec(pool_w.shape, lambda bi, si: (0, 0, 0)),
            pl.BlockSpec((1, d_pool), lambda bi, si: (0, 0)),
        ],
        out_specs=[
            pl.BlockSpec((tm, d_pool), lambda bi, si: (bi * n_tiles + si, 0)),
            pl.BlockSpec((1, blocks_per_tile, d_attn, MOBA_BLOCK), lambda bi, si: (bi, si, 0, 0)),
            pl.BlockSpec((tm, d_attn), lambda bi, si: (bi * n_tiles + si, 0)),
            pl.BlockSpec((1, blocks_per_tile, d_attn, MOBA_BLOCK), lambda bi, si: (bi, si, 0, 0)),
            pl.BlockSpec((blocks_per_tile, 1, d_attn), lambda bi, si: (bi * n_tiles + si, 0, 0)),
            pl.BlockSpec((1, blocks_per_tile, d_attn, MOBA_BLOCK), lambda bi, si: (bi, si, 0, 0)),
        ],
        out_shape=[
            jax.ShapeDtypeStruct((rows, d_pool), _BF16),
            jax.ShapeDtypeStruct((b, n_blocks, d_attn, MOBA_BLOCK), _BF16),
            jax.ShapeDtypeStruct((rows, d_attn), _BF16),
            jax.ShapeDtypeStruct((b, n_blocks, d_attn, MOBA_BLOCK), _BF16),
            jax.ShapeDtypeStruct((b * n_blocks, 1, d_attn), _F32),
            jax.ShapeDtypeStruct((b, n_blocks, d_attn, MOBA_BLOCK), _BF16),
        ],
        scratch_shapes=[pltpu.VMEM((tm + POOL_HALO, d_pool), _F32)],
        compiler_params=pltpu.CompilerParams(
            dimension_semantics=("arbitrary", "arbitrary"), vmem_limit_bytes=VMEM_LIMIT),
        name="proj_pool_rope",
    )
    y_pool, q_t, k_rot, v_t, k_mean, g_attn = proj(
        x2, pos3, freq_tab, w_bf, pool_w.astype(_BF16), pool_scale.reshape(1, d_pool))

    k_mean = k_mean.reshape(b, n_blocks, d_attn)
    attn = pl.pallas_call(
        functools.partial(_attn_kernel, n_blocks=n_blocks, n_heads=n_heads, d_pool=d_pool, alpha=alpha),
        grid=(b, n_blocks),
        in_specs=[
            pl.BlockSpec((1, 1, d_attn, MOBA_BLOCK), lambda bi, i: (bi, i, 0, 0)),
            pl.BlockSpec((s, d_attn), lambda bi, i: (bi, 0)),
            pl.BlockSpec((1, n_blocks, d_attn, MOBA_BLOCK), lambda bi, i: (bi, 0, 0, 0)),
            pl.BlockSpec((1, n_blocks, d_attn), lambda bi, i: (bi, 0, 0)),
            pl.BlockSpec((1, 1, d_attn, MOBA_BLOCK), lambda bi, i: (bi, i, 0, 0)),
            pl.BlockSpec((MOBA_BLOCK, d_pool), lambda bi, i: (bi * n_blocks + i, 0)),
            pl.BlockSpec((MOBA_BLOCK, d_model), lambda bi, i: (bi * n_blocks + i, 0)),
            pl.BlockSpec((d_model, d_model), lambda bi, i: (0, 0)),
            pl.BlockSpec((1, d_model), lambda bi, i: (0, 0)),
            pl.BlockSpec((1, d_model), lambda bi, i: (0, 0)),
        ],
        out_specs=pl.BlockSpec((MOBA_BLOCK, d_model), lambda bi, i: (bi * n_blocks + i, 0)),
        out_shape=jax.ShapeDtypeStruct((rows, d_model), h.dtype),
        scratch_shapes=[
            pltpu.VMEM((n_heads, 2 * HEAD_DIM, MOBA_BLOCK), _BF16),
            pltpu.VMEM((n_heads, n_blocks, MOBA_BLOCK), _F32),
            pltpu.VMEM((n_heads, 1, MOBA_BLOCK), _F32),
            pltpu.VMEM((n_heads, HEAD_DIM + ONES_ROWS, MOBA_BLOCK), _F32),
            pltpu.VMEM((BLOCKS_PER_ITER, n_heads, MOBA_BLOCK, MOBA_BLOCK), _F32),
        ],
        compiler_params=pltpu.CompilerParams(
            dimension_semantics=("arbitrary", "arbitrary"), vmem_limit_bytes=VMEM_LIMIT),
        name="moba_attention",
    )
    out = attn(q_t, k_rot, v_t, k_mean, g_attn, y_pool, x2, w_out.astype(_BF16),
               ln_gain.reshape(1, d_model), ln_bias.reshape(1, d_model))
    return out.reshape(b, s, d_model)


def kernel(x, positions, w_in, pool_w, pool_scale, w_out, ln_gain, ln_bias):
    depth = w_in.shape[0]
    alpha = (2.0 * depth) ** 0.25
    h = x
    for layer in range(depth):
        h = _layer(h, positions, w_in[layer], pool_w[layer], pool_scale[layer],
                   w_out[layer], ln_gain[layer], ln_bias[layer], alpha)
    return h
```
